```python
import jax, jax.numpy as jnp
from jax import lax
import numpy as np

D_MODEL = 4096
BATCH = 8
SEQ = 2048
DEPTH = 2

POOL_WINDOWS = (2, 4, 8, 16)
POOL_GROUPS = 4
POOL_GROUP_DIM = D_MODEL // 16
POOL_WIDTH = POOL_GROUPS * POOL_GROUP_DIM
SB_HEAD_DIM = 128
SB_WIDTH = 3 * D_MODEL // 8
SB_HEADS = SB_WIDTH // SB_HEAD_DIM
SB_BLOCK = 128
HG_DK = 128
HG_DV = 128
HG_WIDTH = 3 * D_MODEL // 8
HG_HEADS = HG_WIDTH // HG_DK
HG_CHUNK = 64
N_BRANCH = 3
MIX_WIDTH = POOL_WIDTH + SB_WIDTH + HG_WIDTH
IN_WIDTH = POOL_WIDTH + 3 * SB_WIDTH + 4 * HG_WIDTH + N_BRANCH * D_MODEL
SPLIT_POINTS = (POOL_WIDTH,
                POOL_WIDTH + SB_WIDTH,
                POOL_WIDTH + 2 * SB_WIDTH,
                POOL_WIDTH + 3 * SB_WIDTH,
                POOL_WIDTH + 3 * SB_WIDTH + HG_WIDTH,
                POOL_WIDTH + 3 * SB_WIDTH + 2 * HG_WIDTH,
                POOL_WIDTH + 3 * SB_WIDTH + 3 * HG_WIDTH,
                POOL_WIDTH + 3 * SB_WIDTH + 4 * HG_WIDTH)
D_FF = 11008
N_EXPERTS = 8
TOP_K = 2
D_EXPERT = 3072
N_DENSE = (DEPTH + 1) // 2
N_MOE = DEPTH // 2
N_MOD = 6
EPS = 1e-6

kernel_name = 'gated_hybrid_pool_stickbreak_hgrn2_moe'


def rms_norm(x, gain):
    xf = x.astype(jnp.float32)
    y = xf * lax.rsqrt(jnp.mean(xf * xf, axis=-1, keepdims=True) + EPS)
    return (y * gain.astype(jnp.float32)).astype(x.dtype)


def swiglu(h, w1, w3, w2):
    return (jax.nn.silu(h @ w1) * (h @ w3)) @ w2


def pool_mixer(u, w_pool, pool_scale):
    B, S, _ = u.shape
    ug = u.astype(jnp.float32).reshape(B, S, POOL_GROUPS, POOL_GROUP_DIM)
    cs = jnp.cumsum(ug, axis=1)
    pos = jnp.arange(S)
    pooled = []
    for g, w in enumerate(POOL_WINDOWS):
        c_g = cs[:, :, g]
        lower = jnp.pad(c_g, ((0, 0), (w, 0), (0, 0)))[:, :S]
        count = jnp.minimum(pos + 1, w).astype(jnp.float32)[None, :, None]
        pooled.append((c_g - lower) / count)
    pooled = jnp.stack(pooled, axis=2) - ug
    mixed = jnp.einsum('bsgc,gcd->bsgd', pooled.astype(u.dtype), w_pool)
    return mixed.reshape(B, S, POOL_WIDTH) * pool_scale


def stick_breaking_attention(q, k, v):
    B, H, S, Dh = q.shape
    scale = Dh ** -0.5
    outs = []
    for blk in range(S // SB_BLOCK):
        t0, t1 = blk * SB_BLOCK, (blk + 1) * SB_BLOCK
        qb, kb, vb = q[:, :, t0:t1], k[:, :, :t1], v[:, :, :t1]
        z = jnp.einsum('bhtd,bhsd->bhts', qb, kb).astype(jnp.float32) * scale
        mask = jnp.arange(t1)[None, :] < jnp.arange(t0, t1)[:, None]
        log_beta = jax.nn.log_sigmoid(z)
        log_keep = jnp.where(mask, jax.nn.log_sigmoid(-z), 0.0)
        after = lax.cumsum(log_keep, axis=3, reverse=True) - log_keep
        a = jnp.where(mask, jnp.exp(log_beta + after), 0.0)
        outs.append(jnp.einsum('bhts,bhsd->bhtd', a.astype(v.dtype), vb))
    return jnp.concatenate(outs, axis=2)


def hgrn2_recurrence(q, k, i, log_f):
    B, S, H, DK = q.shape
    DV = i.shape[-1]
    N = S // HG_CHUNK

    def to_chunks(t):
        return t.astype(jnp.float32).reshape(B, N, HG_CHUNK, H, t.shape[-1]).transpose(1, 0, 3, 2, 4)

    qc, kc, ic, fc = to_chunks(q), to_chunks(k), to_chunks(i), to_chunks(log_f)
    causal = jnp.tril(jnp.ones((HG_CHUNK, HG_CHUNK), dtype=bool))[None, None, :, :, None]

    def step(state, xs):
        qx, kx, ix, fx = xs
        b = jnp.cumsum(fx, axis=2)
        diff = b[:, :, :, None, :] - b[:, :, None, :, :]
        decay = jnp.exp(jnp.where(causal, diff, -jnp.inf))
        scores = jnp.einsum('bhtk,bhsk,bhtsk->bhts', qx, kx, decay)
        o = (jnp.einsum('bhts,bhsv->bhtv', scores, ix)
             + jnp.einsum('bhtk,bhkv->bhtv', qx * jnp.exp(b), state))
        b_last = b[:, :, -1:, :]
        state = (jnp.exp(b_last[:, :, 0, :, None]) * state
                 + jnp.einsum('bhsk,bhsv->bhkv', kx * jnp.exp(b_last - b), ix))
        return state, o

    s0 = jnp.zeros((B, H, DK, DV), jnp.float32)
    _, o = lax.scan(step, s0, (qc, kc, ic, fc))
    return o.transpose(1, 0, 3, 2, 4).reshape(B, S, H, DV)


def hybrid_mixer(h, w_in, w_pool, pool_scale, lower_bound, hg_norm, w_br_pool, w_br_sb, w_br_hg, w_out):
    B, S, _ = h.shape
    proj = h @ w_in
    u_pool, q_sb, k_sb, v_sb, q_hg, f_hg, i_hg, g_hg, gate_logits = jnp.split(proj, SPLIT_POINTS, axis=-1)

    y_pool = pool_mixer(u_pool, w_pool, pool_scale)

    def heads(t):
        return t.reshape(B, S, SB_HEADS, SB_HEAD_DIM).transpose(0, 2, 1, 3)
    o_sb = stick_breaking_attention(heads(q_sb), heads(k_sb), heads(v_sb))
    y_sb = o_sb.transpose(0, 2, 1, 3).reshape(B, S, SB_WIDTH)

    f = lower_bound + (1.0 - lower_bound) * jax.nn.sigmoid(f_hg.astype(jnp.float32))
    log_f = jnp.log(f)
    k_hg = 1.0 - f
    q_act = jax.nn.silu(q_hg)
    o_hg = hgrn2_recurrence(q_act.reshape(B, S, HG_HEADS, HG_DK),
                            k_hg.reshape(B, S, HG_HEADS, HG_DK),
                            i_hg.reshape(B, S, HG_HEADS, HG_DV),
                            log_f.reshape(B, S, HG_HEADS, HG_DK)).astype(h.dtype)
    o_hg = rms_norm(o_hg, hg_norm) * jax.nn.silu(g_hg.reshape(B, S, HG_HEADS, HG_DV))
    y_hg = o_hg.reshape(B, S, HG_WIDTH)

    gate = jax.nn.sigmoid(gate_logits.reshape(B, S, N_BRANCH, D_MODEL))
    merged = (gate[:, :, 0] * (y_pool @ w_br_pool)
              + gate[:, :, 1] * (y_sb @ w_br_sb)
              + gate[:, :, 2] * (y_hg @ w_br_hg))
    return merged @ w_out


def moe_swiglu(h, w_router, b_router, w1, w3, w2):
    logits = (h @ w_router).astype(jnp.float32) + b_router.astype(jnp.float32)
    top_v, top_i = lax.top_k(logits, TOP_K)
    top_w = jax.nn.softmax(top_v, axis=-1)
    combine = jnp.sum(jax.nn.one_hot(top_i, N_EXPERTS, dtype=jnp.float32) * top_w[..., None], axis=-2)
    y = jnp.zeros_like(h)
    for e in range(N_EXPERTS):
        y = y + combine[..., e:e + 1].astype(h.dtype) * swiglu(h, w1[e], w3[e], w2[e])
    return y


def setup_inputs(seed: int = 0) -> dict:
    key = jax.random.key(seed)
    ks = jax.random.split(key, 32)
    f32 = jnp.float32

    def nrm(k, shape, scale):
        return jax.random.normal(k, shape, f32) * scale

    def gain(k, shape):
        return 1.0 + 0.02 * jax.random.normal(k, shape, f32)

    D = D_MODEL
    return {
        'x': nrm(ks[0], (BATCH, SEQ, D), 1.0),
        'c': nrm(ks[1], (BATCH, D), 1.0),
        'w_ada': nrm(ks[2], (DEPTH, D, N_MOD * D), 0.5 * D ** -0.5),
        'b_ada': nrm(ks[3], (DEPTH, N_MOD * D), 0.02),
        'norm_mix': gain(ks[4], (DEPTH, D)),
        'norm_ffn': gain(ks[5], (DEPTH, D)),
        'w_in': nrm(ks[6], (DEPTH, D, IN_WIDTH), D ** -0.5),
        'w_pool': nrm(ks[7], (DEPTH, POOL_GROUPS, POOL_GROUP_DIM, POOL_GROUP_DIM), POOL_GROUP_DIM ** -0.5),
        'pool_scale': gain(ks[8], (DEPTH, POOL_WIDTH)),
        'lb_logits': nrm(ks[9], (DEPTH, HG_WIDTH), 0.5),
        'hg_norm': gain(ks[10], (DEPTH, HG_DV)),
        'w_br_pool': nrm(ks[11], (DEPTH, POOL_WIDTH, D), POOL_WIDTH ** -0.5),
        'w_br_sb': nrm(ks[12], (DEPTH, SB_WIDTH, D), SB_WIDTH ** -0.5),
        'w_br_hg': nrm(ks[13], (DEPTH, HG_WIDTH, D), HG_WIDTH ** -0.5),
        'w_out': nrm(ks[14], (DEPTH, D, D), D ** -0.5),
        'ffn_w1': nrm(ks[15], (N_DENSE, D, D_FF), D ** -0.5),
        'ffn_w3': nrm(ks[16], (N_DENSE, D, D_FF), D ** -0.5),
        'ffn_w2': nrm(ks[17], (N_DENSE, D_FF, D), D_FF ** -0.5),
        'w_router': nrm(ks[18], (N_MOE, D, N_EXPERTS), D ** -0.5),
        'b_router': nrm(ks[19], (N_MOE, N_EXPERTS), 0.01),
        'moe_w1': nrm(ks[20], (N_MOE, N_EXPERTS, D, D_EXPERT), D ** -0.5),
        'moe_w3': nrm(ks[21], (N_MOE, N_EXPERTS, D, D_EXPERT), D ** -0.5),
        'moe_w2': nrm(ks[22], (N_MOE, N_EXPERTS, D_EXPERT, D), D_EXPERT ** -0.5),
        'final_norm': gain(ks[23], (D,)),
    }


def reference(x, c, w_ada, b_ada, norm_mix, norm_ffn, w_in, w_pool, pool_scale, lb_logits, hg_norm,
              w_br_pool, w_br_sb, w_br_hg, w_out, ffn_w1, ffn_w3, ffn_w2, w_router, b_router,
              moe_w1, moe_w3, moe_w2, final_norm):
    lb_p = jax.nn.softmax(lb_logits.astype(jnp.float32), axis=0)
    lower_bounds = jnp.cumsum(lb_p, axis=0) - lb_p[0:1]
    c_act = jax.nn.silu(c)
    for l in range(DEPTH):
        mod = c_act @ w_ada[l] + b_ada[l]
        sh1, sc1, gt1, sh2, sc2, gt2 = jnp.split(mod[:, None, :], N_MOD, axis=-1)
        h = rms_norm(x, norm_mix[l]) * (1.0 + sc1) + sh1
        x = x + gt1 * hybrid_mixer(h, w_in[l], w_pool[l], pool_scale[l], lower_bounds[l], hg_norm[l],
                                   w_br_pool[l], w_br_sb[l], w_br_hg[l], w_out[l])
        h = rms_norm(x, norm_ffn[l]) * (1.0 + sc2) + sh2
        if l % 2 == 0:
            j = l // 2
            f_out = swiglu(h, ffn_w1[j], ffn_w3[j], ffn_w2[j])
        else:
            j = l // 2
            f_out = moe_swiglu(h, w_router[j], b_router[j], moe_w1[j], moe_w3[j], moe_w2[j])
        x = x + gt2 * f_out
    return rms_norm(x, final_norm)
```

```python
import functools
import math

import jax
import jax.numpy as jnp
from jax import lax
from jax.experimental import pallas as pl
from jax.experimental.pallas import tpu as pltpu

F32 = jnp.float32
BF16 = jnp.bfloat16
I32 = jnp.int32

EPS = 1e-6
HEAD = 128
POOL_WINDOWS = (2, 4, 8, 16)
POOL_HALO = 16
HG_CHUNK = 64
N_MOD = 6
TOP_K = 2
LANES = 128
SUBLANES = 8
VMEM_CAP = 60000 * 1024
ACT_DTYPE = BF16


def _tile(n, pref, mult=LANES):
    if n <= pref:
        return n
    t = (pref // mult) * mult
    while t >= mult:
        if n % t == 0:
            return t
        t -= mult
    return n


def _nbytes(shape, dtype):
    n = 1
    for s in shape:
        n *= s
    return n * jnp.dtype(dtype).itemsize


def _params(sems, blocks, scratch=()):
    need = 2 * sum(_nbytes(s, d) for s, d in blocks) + sum(_nbytes(s, d) for s, d in scratch)
    limit = min(VMEM_CAP, max(32 * 1024 * 1024, need + need // 4 + (4 << 20)))
    return pltpu.CompilerParams(dimension_semantics=sems, vmem_limit_bytes=int(limit))


def _split3(x):
    hi = x.astype(BF16)
    r1 = x - hi.astype(F32)
    mid = r1.astype(BF16)
    lo = (r1 - mid.astype(F32)).astype(BF16)
    return hi, mid, lo


def _dot(a, b):
    return jnp.dot(a, b, preferred_element_type=F32)


def _ada_kernel(c_ref, w_ref, b_ref, o_ref):
    ca = jax.nn.silu(c_ref[...])
    o_ref[0] = _dot(ca.astype(BF16), w_ref[0]) + b_ref[0]


def _ada_mod(c, w_ada, b_ada):
    L, D, N = w_ada.shape
    B = c.shape[0]
    tn = _tile(N, 1024)
    blocks = [((B, D), F32), ((1, D, tn), BF16), ((1, 1, tn), F32), ((1, B, tn), F32)]
    return pl.pallas_call(
        _ada_kernel,
        grid=(L, N // tn),
        in_specs=[pl.BlockSpec((B, D), lambda l, j: (0, 0)),
                  pl.BlockSpec((1, D, tn), lambda l, j: (l, 0, j)),
                  pl.BlockSpec((1, 1, tn), lambda l, j: (l, 0, j))],
        out_specs=pl.BlockSpec((1, B, tn), lambda l, j: (l, 0, j)),
        out_shape=jax.ShapeDtypeStruct((L, B, N), F32),
        compiler_params=_params(("parallel", "parallel"), blocks),
        name="ada_mod",
    )(c, w_ada, b_ada.reshape(L, 1, N))


def _norm_mod_value(x_ref, g_ref, sc_ref, sh_ref):
    x = x_ref[...]
    ms = jnp.mean(x * x, axis=-1, keepdims=True)
    y = x * lax.rsqrt(ms + EPS) * g_ref[...]
    return y * (1.0 + sc_ref[0]) + sh_ref[0]


def _norm_mod_kernel(x_ref, g_ref, sc_ref, sh_ref, o_ref):
    o_ref[...] = _norm_mod_value(x_ref, g_ref, sc_ref, sh_ref).astype(o_ref.dtype)


def _norm_mod_router_kernel(x_ref, g_ref, sc_ref, sh_ref, wr_ref, br_ref, o_ref, idx_ref, wgt_ref, *, n_experts):
    h = _norm_mod_value(x_ref, g_ref, sc_ref, sh_ref)
    o_ref[...] = h.astype(o_ref.dtype)
    h_hi = h.astype(BF16)
    h_lo = (h - h_hi.astype(F32)).astype(BF16)
    logits = (_dot(h_hi, wr_ref[0]) + _dot(h_hi, wr_ref[1]) + _dot(h_lo, wr_ref[0])
              + _dot(h_hi, wr_ref[2]) + _dot(h_lo, wr_ref[1])) + br_ref[...]
    lane = lax.broadcasted_iota(I32, logits.shape, 1)
    lanef = lane.astype(F32)
    lg = jnp.where(lane < n_experts, logits, -jnp.inf)
    m1 = jnp.max(lg, axis=-1, keepdims=True)
    i1 = jnp.min(jnp.where(lg == m1, lanef, float(LANES)), axis=-1, keepdims=True)
    lg2 = jnp.where(lanef == i1, -jnp.inf, lg)
    m2 = jnp.max(lg2, axis=-1, keepdims=True)
    i2 = jnp.min(jnp.where(lg2 == m2, lanef, float(LANES)), axis=-1, keepdims=True)
    e2 = jnp.exp(m2 - m1)
    w1 = 1.0 / (1.0 + e2)
    w2 = e2 / (1.0 + e2)
    idx_ref[...] = jnp.where(lane == 0, i1, jnp.where(lane == 1, i2, 0.0)).astype(I32)
    wgt_ref[...] = jnp.where(lane == 0, w1, jnp.where(lane == 1, w2, 0.0))


def _norm_mod(x2, gain, mod_l, sc_blk, sh_blk, S, router=None):
    T, D = x2.shape
    tm = _tile(S, 256, SUBLANES)
    nps = S // tm
    in_specs = [pl.BlockSpec((tm, D), lambda i: (i, 0)),
                pl.BlockSpec((1, D), lambda i: (0, 0)),
                pl.BlockSpec((1, 1, D), lambda i: (i // nps, 0, sc_blk)),
                pl.BlockSpec((1, 1, D), lambda i: (i // nps, 0, sh_blk))]
    blocks = [((tm, D), F32), ((1, D), F32), ((1, 1, D), F32), ((1, 1, D), F32), ((tm, D), ACT_DTYPE)]
    h_spec = pl.BlockSpec((tm, D), lambda i: (i, 0))
    h_shape = jax.ShapeDtypeStruct((T, D), ACT_DTYPE)
    if router is None:
        return pl.pallas_call(
            _norm_mod_kernel, grid=(T // tm,), in_specs=in_specs, out_specs=h_spec, out_shape=h_shape,
            compiler_params=_params(("parallel",), blocks), name="norm_mod",
        )(x2, gain.reshape(1, D), mod_l, mod_l)
    w_router, b_router = router
    E = w_router.shape[1]
    wr = jnp.zeros((D, LANES), F32).at[:, :E].set(w_router)
    wr3 = jnp.stack(_split3(wr))
    br = jnp.zeros((1, LANES), F32).at[0, :E].set(b_router)
    in_specs += [pl.BlockSpec((3, D, LANES), lambda i: (0, 0, 0)), pl.BlockSpec((1, LANES), lambda i: (0, 0))]
    blocks += [((3, D, LANES), BF16), ((tm, LANES), F32), ((tm, LANES), F32), ((tm, D), F32)]
    lane_spec = pl.BlockSpec((tm, LANES), lambda i: (i, 0))
    return pl.pallas_call(
        functools.partial(_norm_mod_router_kernel, n_experts=E),
        grid=(T // tm,), in_specs=in_specs,
        out_specs=[h_spec, lane_spec, lane_spec],
        out_shape=[h_shape, jax.ShapeDtypeStruct((T, LANES), I32), jax.ShapeDtypeStruct((T, LANES), F32)],
        compiler_params=_params(("parallel",), blocks), name="norm_mod_router",
    )(x2, gain.reshape(1, D), mod_l, mod_l, wr3, br)


def _final_norm_kernel(x_ref, g_ref, o_ref):
    x = x_ref[...]
    ms = jnp.mean(x * x, axis=-1, keepdims=True)
    o_ref[...] = x * lax.rsqrt(ms + EPS) * g_ref[...]


def _final_norm(x2, gain):
    T, D = x2.shape
    tm = _tile(T, 256, SUBLANES)
    blocks = [((tm, D), F32), ((1, D), F32), ((tm, D), F32)]
    return pl.pallas_call(
        _final_norm_kernel, grid=(T // tm,),
        in_specs=[pl.BlockSpec((tm, D), lambda i: (i, 0)), pl.BlockSpec((1, D), lambda i: (0, 0))],
        out_specs=pl.BlockSpec((tm, D), lambda i: (i, 0)),
        out_shape=jax.ShapeDtypeStruct((T, D), F32),
        compiler_params=_params(("parallel",), blocks), name="final_norm",
    )(x2, gain.reshape(1, D))


def _proj_kernel(a_ref, w_ref, o_ref):
    o_ref[...] = _dot(a_ref[...], w_ref[...]).astype(o_ref.dtype)


def _proj_matmul(a, w, out_dtype):
    M, K = a.shape
    N = w.shape[1]
    tm, tn = _tile(M, 1024), _tile(N, 512)
    blocks = [((tm, K), a.dtype), ((K, tn), w.dtype), ((tm, tn), out_dtype)]
    return pl.pallas_call(
        _proj_kernel, grid=(M // tm, N // tn),
        in_specs=[pl.BlockSpec((tm, K), lambda i, j: (i, 0)), pl.BlockSpec((K, tn), lambda i, j: (0, j))],
        out_specs=pl.BlockSpec((tm, tn), lambda i, j: (i, j)),
        out_shape=jax.ShapeDtypeStruct((M, N), out_dtype),
        compiler_params=_params(("parallel", "arbitrary"), blocks), name="proj_matmul",
    )(a, w)


def _swiglu_kernel(a_ref, w1_ref, w3_ref, o_ref):
    a = a_ref[...]
    o_ref[...] = (jax.nn.silu(_dot(a, w1_ref[...])) * _dot(a, w3_ref[...])).astype(o_ref.dtype)


def _swiglu_matmul(a, w1, w3):
    M, K = a.shape
    N = w1.shape[1]
    tm, tn = _tile(M, 1024), _tile(N, 256)
    blocks = [((tm, K), a.dtype), ((K, tn), BF16), ((K, tn), BF16), ((tm, tn), ACT_DTYPE)]
    return pl.pallas_call(
        _swiglu_kernel, grid=(M // tm, N // tn),
        in_specs=[pl.BlockSpec((tm, K), lambda i, j: (i, 0)),
                  pl.BlockSpec((K, tn), lambda i, j: (0, j)),
                  pl.BlockSpec((K, tn), lambda i, j: (0, j))],
        out_specs=pl.BlockSpec((tm, tn), lambda i, j: (i, j)),
        out_shape=jax.ShapeDtypeStruct((M, N), ACT_DTYPE),
        compiler_params=_params(("parallel", "arbitrary"), blocks), name="swiglu_matmul",
    )(a, w1, w3)


def _resid_kernel(a_ref, w_ref, x_ref, gt_ref, o_ref, acc_ref, *, nk):
    part = _dot(a_ref[...], w_ref[...])
    if nk == 1:
        o_ref[...] = x_ref[...] + gt_ref[0] * part
        return
    k = pl.program_id(2)

    @pl.when(k == 0)
    def _():
        acc_ref[...] = part

    @pl.when(jnp.logical_and(k > 0, k < nk - 1))
    def _():
        acc_ref[...] += part

    @pl.when(k == nk - 1)
    def _():
        o_ref[...] = x_ref[...] + gt_ref[0] * (acc_ref[...] + part)


def _resid_matmul(a, w, x2, mod_l, gt_blk, S, tk_pref=4096):
    M, K = a.shape
    N = w.shape[1]
    tk = _tile(K, tk_pref)
    nk = K // tk
    tm, tn = _tile(S, 1024), _tile(N, 512)
    nps = S // tm
    gpb = N // tn
    blocks = [((tm, tk), a.dtype), ((tk, tn), BF16), ((tm, tn), F32), ((1, 1, tn), F32), ((tm, tn), F32)]
    scratch = [((tm, tn), F32)]
    return pl.pallas_call(
        functools.partial(_resid_kernel, nk=nk), grid=(M // tm, N // tn, nk),
        in_specs=[pl.BlockSpec((tm, tk), lambda i, j, k: (i, k)),
                  pl.BlockSpec((tk, tn), lambda i, j, k: (k, j)),
                  pl.BlockSpec((tm, tn), lambda i, j, k: (i, j)),
                  pl.BlockSpec((1, 1, tn), lambda i, j, k: (i // nps, 0, gt_blk * gpb + j))],
        out_specs=pl.BlockSpec((tm, tn), lambda i, j, k: (i, j)),
        out_shape=jax.ShapeDtypeStruct((M, N), F32),
        scratch_shapes=[pltpu.VMEM((tm, tn), F32)],
        compiler_params=_params(("parallel", "arbitrary", "arbitrary"), blocks, scratch), name="resid_matmul",
    )(a, w, x2, mod_l)


def _merge_kernel(yp_ref, ys_ref, yh_ref, wp_ref, ws_ref, wh_ref, g0_ref, g1_ref, g2_ref, o_ref):
    def gate(g_ref):
        return jax.nn.sigmoid(g_ref[...].astype(F32))
    m = (gate(g0_ref) * _dot(yp_ref[...], wp_ref[...])
         + gate(g1_ref) * _dot(ys_ref[...], ws_ref[...])
         + gate(g2_ref) * _dot(yh_ref[...], wh_ref[...]))
    o_ref[...] = m.astype(o_ref.dtype)


def _merge_matmul(y_pool, y_sb, y_hg, w_p, w_s, w_h, proj, gate_col0):
    M = y_pool.shape[0]
    D = w_p.shape[1]
    tm, tn = _tile(M, 1024), _tile(math.gcd(D, gate_col0), 512)
    g0 = gate_col0 // tn
    gpb = D // tn
    ks = (y_pool.shape[1], y_sb.shape[1], y_hg.shape[1])
    blocks = ([((tm, k), ACT_DTYPE) for k in ks] + [((k, tn), BF16) for k in ks]
              + [((tm, tn), proj.dtype)] * 3 + [((tm, tn), ACT_DTYPE)])
    a_specs = [pl.BlockSpec((tm, k), lambda i, j: (i, 0)) for k in ks]
    w_specs = [pl.BlockSpec((k, tn), lambda i, j: (0, j)) for k in ks]
    g_specs = [pl.BlockSpec((tm, tn), functools.partial(lambda i, j, b: (i, g0 + b * gpb + j), b=b)) for b in range(3)]
    return pl.pallas_call(
        _merge_kernel, grid=(M // tm, D // tn),
        in_specs=a_specs + w_specs + g_specs,
        out_specs=pl.BlockSpec((tm, tn), lambda i, j: (i, j)),
        out_shape=jax.ShapeDtypeStruct((M, D), ACT_DTYPE),
        compiler_params=_params(("parallel", "arbitrary"), blocks), name="merge_matmul",
    )(y_pool, y_sb, y_hg, w_p, w_s, w_h, proj, proj, proj)


def _pool_kernel(u_ref, wp_ref, ps_ref, o_ref, ext_ref, *, ts, pc):
    si = pl.program_id(1)

    @pl.when(si == 0)
    def _():
        ext_ref[0:POOL_HALO, :] = jnp.zeros((POOL_HALO, ext_ref.shape[1]), F32)

    u = u_ref[...].astype(F32)
    ext_ref[POOL_HALO:POOL_HALO + ts, :] = u
    pos = si * ts + lax.broadcasted_iota(I32, (ts, 1), 0)
    for g, w in enumerate(POOL_WINDOWS):
        c0, c1 = g * pc, (g + 1) * pc
        ug = u[:, c0:c1]
        acc = ug
        for j in range(1, w):
            acc = acc + ext_ref[POOL_HALO - j:POOL_HALO - j + ts, c0:c1]
        cnt = jnp.minimum(pos + 1, w).astype(F32)
        pooled = acc / cnt - ug
        y = _dot(pooled.astype(BF16), wp_ref[g]) * ps_ref[:, c0:c1]
        o_ref[:, c0:c1] = y.astype(o_ref.dtype)
    ext_ref[0:POOL_HALO, :] = ext_ref[ts:ts + POOL_HALO, :]


def _pool_mixer(proj, w_pool, pool_scale, B, S):
    G, pc, _ = w_pool.shape
    pw = G * pc
    ts = _tile(S, 512, POOL_HALO)
    ns = S // ts
    blocks = [((ts, pw), proj.dtype), ((G, pc, pc), BF16), ((1, pw), F32), ((ts, pw), ACT_DTYPE)]
    scratch = [((ts + POOL_HALO, pw), F32)]
    return pl.pallas_call(
        functools.partial(_pool_kernel, ts=ts, pc=pc), grid=(B, ns),
        in_specs=[pl.BlockSpec((ts, pw), lambda b, s: (b * ns + s, 0)),
                  pl.BlockSpec((G, pc, pc), lambda b, s: (0, 0, 0)),
                  pl.BlockSpec((1, pw), lambda b, s: (0, 0))],
        out_specs=pl.BlockSpec((ts, pw), lambda b, s: (b * ns + s, 0)),
        out_shape=jax.ShapeDtypeStruct((B * S, pw), ACT_DTYPE),
        scratch_shapes=[pltpu.VMEM((ts + POOL_HALO, pw), F32)],
        compiler_params=_params(("parallel", "arbitrary"), blocks, scratch), name="pool_mixer",
    )(proj, w_pool, pool_scale.reshape(1, pw))


def _sb_kernel(q_ref, k_ref, v_ref, o_ref, *, tq):
    qi = pl.program_id(2)
    q = q_ref[...].astype(BF16)
    scale = HEAD ** -0.5
    rows = lax.broadcasted_iota(I32, (tq, tq), 0)
    cols = lax.broadcasted_iota(I32, (tq, tq), 1)
    later = (rows > cols).astype(BF16)

    def step(jj, carry):
        acc, run = carry
        kj = qi - jj
        k0 = pl.multiple_of(kj * tq, tq)
        ks = k_ref[pl.ds(k0, tq), :].astype(BF16)
        vs = v_ref[pl.ds(k0, tq), :].astype(BF16)
        z = lax.dot_general(q, ks, (((1,), (1,)), ((), ())), preferred_element_type=F32) * scale
        log_beta = jnp.minimum(z, 0.0) - jnp.log1p(jnp.exp(-jnp.abs(z)))
        mask = (kj * tq + cols) < (qi * tq + rows)
        log_keep = jnp.where(mask, log_beta - z, 0.0)
        hi, mid, lo = _split3(log_keep)
        after = _dot(hi, later) + _dot(mid, later) + _dot(lo, later) + run
        a = jnp.where(mask, jnp.exp(log_beta + after), 0.0)
        acc = acc + _dot(a.astype(BF16), vs)
        run = run + jnp.sum(log_keep, axis=1, keepdims=True)
        return acc, run

    acc, _ = lax.fori_loop(0, qi + 1, step, (jnp.zeros((tq, HEAD), F32), jnp.zeros((tq, 1), F32)))
    o_ref[...] = acc.astype(o_ref.dtype)


def _sb_attention(proj, q_col0, k_col0, v_col0, n_heads, B, S):
    tq = _tile(S, 256)
    nq = S // tq
    qb, kb, vb = q_col0 // HEAD, k_col0 // HEAD, v_col0 // HEAD
    blocks = [((tq, HEAD), proj.dtype), ((S, HEAD), proj.dtype), ((S, HEAD), proj.dtype), ((tq, HEAD), ACT_DTYPE)]
    return pl.pallas_call(
        functools.partial(_sb_kernel, tq=tq), grid=(B, n_heads, nq),
        in_specs=[pl.BlockSpec((tq, HEAD), lambda b, h, i: (b * nq + i, qb + h)),
                  pl.BlockSpec((S, HEAD), lambda b, h, i: (b, kb + h)),
                  pl.BlockSpec((S, HEAD), lambda b, h, i: (b, vb + h))],
        out_specs=pl.BlockSpec((tq, HEAD), lambda b, h, i: (b * nq + i, h)),
        out_shape=jax.ShapeDtypeStruct((B * S, n_heads * HEAD), ACT_DTYPE),
        compiler_params=_params(("parallel", "parallel", "arbitrary"), blocks), name="sb_attention",
    )(proj, proj, proj)


def _hg_kernel(q_ref, f_ref, i_ref, g_ref, lbl_ref, gn_ref, o_ref, st_ref, b_ref, kk_ref, *, layer, n_chunks):
    C = HG_CHUNK
    si = pl.program_id(2)

    @pl.when(si == 0)
    def _():
        st_ref[...] = jnp.zeros(st_ref.shape, F32)

    lbl = lbl_ref[...]
    e = jnp.exp(lbl - jnp.max(lbl, axis=0, keepdims=True))
    p = e / jnp.sum(e, axis=0, keepdims=True)
    lb = jnp.zeros((1, HEAD), F32)
    for j in range(1, layer + 1):
        lb = lb + p[j:j + 1, :]
    gn = gn_ref[...]
    rows = lax.broadcasted_iota(I32, (C, C), 0)
    cols = lax.broadcasted_iota(I32, (C, C), 1)
    incl = (rows >= cols).astype(BF16)
    lane = lax.broadcasted_iota(I32, (SUBLANES, C), 1)
    sub = lax.broadcasted_iota(I32, (SUBLANES, HEAD), 0)

    def chunk(ci, carry):
        r0 = pl.multiple_of(ci * C, C)
        qh = q_ref[pl.ds(r0, C), :].astype(F32)
        fh = f_ref[pl.ds(r0, C), :].astype(F32)
        ih = i_ref[pl.ds(r0, C), :].astype(BF16)
        gh = g_ref[pl.ds(r0, C), :].astype(F32)
        f = lb + (1.0 - lb) * jax.nn.sigmoid(fh)
        kk = 1.0 - f
        qa = jax.nn.silu(qh)
        hi, mid, lo = _split3(jnp.log(f))
        b = _dot(incl, hi) + _dot(incl, mid) + _dot(incl, lo)
        b_ref[...] = b
        kk_ref[...] = kk
        pieces = []
        for g in range(C // SUBLANES):
            t0 = g * SUBLANES
            bt = b[t0:t0 + SUBLANES, :]
            qt = qa[t0:t0 + SUBLANES, :]
            sc = jnp.zeros((SUBLANES, C), F32)
            for s in range(t0 + SUBLANES):
                d = bt - b_ref[pl.ds(s, 1), :]
                if s >= t0:
                    d = jnp.where(sub + t0 >= s, d, -jnp.inf)
                wgt = qt * kk_ref[pl.ds(s, 1), :] * jnp.exp(d)
                col = jnp.sum(wgt, axis=1, keepdims=True)
                sc = jnp.where(lane == s, col, sc)
            pieces.append(sc)
        scores = jnp.concatenate(pieces, axis=0)
        st = st_ref[...]
        o = _dot(scores.astype(BF16), ih)
        qd = (qa * jnp.exp(b)).astype(BF16)
        o = o + lax.dot_general(qd, st.astype(BF16), (((1,), (1,)), ((), ())), preferred_element_type=F32)
        b_last = b[C - 1:C, :]
        kd = (kk * jnp.exp(b_last - b)).astype(BF16)
        st_ref[...] = st * jnp.exp(b_last) + lax.dot_general(ih, kd, (((0,), (0,)), ((), ())),
                                                              preferred_element_type=F32)
        y = o * lax.rsqrt(jnp.mean(o * o, axis=-1, keepdims=True) + EPS) * gn
        o_ref[pl.ds(r0, C), :] = (y * jax.nn.silu(gh)).astype(o_ref.dtype)
        return carry

    lax.fori_loop(0, n_chunks, chunk, 0)


def _hgrn2(proj, q_col0, f_col0, i_col0, g_col0, lb_logits, hg_norm, layer, n_heads, B, S):
    ts = _tile(S, 512, HG_CHUNK)
    ns = S // ts
    L = lb_logits.shape[0]
    qb, fb, ib, gb = q_col0 // HEAD, f_col0 // HEAD, i_col0 // HEAD, g_col0 // HEAD

    def col_spec(cb):
        return pl.BlockSpec((ts, HEAD), lambda b, h, s: (b * ns + s, cb + h))

    blocks = [((ts, HEAD), proj.dtype)] * 4 + [((L, HEAD), F32), ((1, HEAD), F32), ((ts, HEAD), ACT_DTYPE)]
    scratch = [((HEAD, HEAD), F32), ((HG_CHUNK, HEAD), F32), ((HG_CHUNK, HEAD), F32)]
    return pl.pallas_call(
        functools.partial(_hg_kernel, layer=layer, n_chunks=ts // HG_CHUNK), grid=(B, n_heads, ns),
        in_specs=[col_spec(qb), col_spec(fb), col_spec(ib), col_spec(gb),
                  pl.BlockSpec((L, HEAD), lambda b, h, s: (0, h)),
                  pl.BlockSpec((1, HEAD), lambda b, h, s: (0, 0))],
        out_specs=pl.BlockSpec((ts, HEAD), lambda b, h, s: (b * ns + s, h)),
        out_shape=jax.ShapeDtypeStruct((B * S, n_heads * HEAD), ACT_DTYPE),
        scratch_shapes=[pltpu.VMEM(s, d) for s, d in scratch],
        compiler_params=_params(("parallel", "parallel", "arbitrary"), blocks, scratch), name="hgrn2",
    )(proj, proj, proj, proj, lb_logits, hg_norm.reshape(1, HEAD))


def _gather_rows_kernel(tok_ref, h_hbm, o_hbm, sem, *, rows):
    base = pl.program_id(0) * rows

    def issue(r, c):
        pltpu.make_async_copy(h_hbm.at[tok_ref[base + r]], o_hbm.at[base + r], sem).start()
        return c

    lax.fori_loop(0, rows, issue, 0)

    def drain(r, c):
        pltpu.make_async_copy(h_hbm.at[0], o_hbm.at[base + r], sem).wait()
        return c

    lax.fori_loop(0, rows, drain, 0)


def _gather_rows(h3, row_tok, rows_per_step):
    P = row_tok.shape[0]
    return pl.pallas_call(
        functools.partial(_gather_rows_kernel, rows=rows_per_step),
        grid_spec=pltpu.PrefetchScalarGridSpec(
            num_scalar_prefetch=1, grid=(P // rows_per_step,),
            in_specs=[pl.BlockSpec(memory_space=pl.ANY)],
            out_specs=pl.BlockSpec(memory_space=pl.ANY),
            scratch_shapes=[pltpu.SemaphoreType.DMA(())]),
        out_shape=jax.ShapeDtypeStruct((P,) + h3.shape[1:], h3.dtype),
        compiler_params=pltpu.CompilerParams(dimension_semantics=("arbitrary",)),
        name="gather_rows",
    )(row_tok, h3)


def _moe_swiglu_kernel(te_ref, nt_ref, a_ref, w1_ref, w3_ref, o_ref):
    active = pl.program_id(0) < nt_ref[0]

    @pl.when(active)
    def _():
        a = a_ref[...]
        o_ref[...] = (jax.nn.silu(_dot(a, w1_ref[0])) * _dot(a, w3_ref[0])).astype(o_ref.dtype)

    @pl.when(jnp.logical_not(active))
    def _():
        o_ref[...] = jnp.zeros(o_ref.shape, o_ref.dtype)


def _moe_swiglu(hs, w1, w3, tile_expert, n_tiles, tmg):
    P, K = hs.shape
    N = w1.shape[2]
    tn = _tile(N, 512)
    blocks = [((tmg, K), hs.dtype), ((1, K, tn), BF16), ((1, K, tn), BF16), ((tmg, tn), ACT_DTYPE)]

    def a_map(i, j, te, nt):
        return (jnp.minimum(i, nt[0] - 1), 0)

    def w_map(i, j, te, nt):
        return (te[i], 0, jnp.where(i < nt[0], j, N // tn - 1))

    return pl.pallas_call(
        _moe_swiglu_kernel,
        grid_spec=pltpu.PrefetchScalarGridSpec(
            num_scalar_prefetch=2, grid=(P // tmg, N // tn),
            in_specs=[pl.BlockSpec((tmg, K), a_map), pl.BlockSpec((1, K, tn), w_map), pl.BlockSpec((1, K, tn), w_map)],
            out_specs=pl.BlockSpec((tmg, tn), lambda i, j, te, nt: (i, j))),
        out_shape=jax.ShapeDtypeStruct((P, N), ACT_DTYPE),
        compiler_params=_params(("arbitrary", "arbitrary"), blocks), name="moe_swiglu",
    )(tile_expert, n_tiles, hs, w1, w3)


def _moe_down_kernel(te_ref, nt_ref, a_ref, w_ref, rw_ref, o_ref):
    active = pl.program_id(0) < nt_ref[0]

    @pl.when(active)
    def _():
        o_ref[...] = _dot(a_ref[...], w_ref[0]) * rw_ref[...]

    @pl.when(jnp.logical_not(active))
    def _():
        o_ref[...] = jnp.zeros(o_ref.shape, o_ref.dtype)


def _moe_down(a, w2, row_w, tile_expert, n_tiles, tmg):
    P, K = a.shape
    N = w2.shape[2]
    tn = _tile(N, 1024)
    blocks = [((tmg, K), a.dtype), ((1, K, tn), BF16), ((tmg, LANES), F32), ((tmg, tn), F32)]

    def a_map(i, j, te, nt):
        return (jnp.minimum(i, nt[0] - 1), 0)

    def w_map(i, j, te, nt):
        return (te[i], 0, jnp.where(i < nt[0], j, N // tn - 1))

    return pl.pallas_call(
        _moe_down_kernel,
        grid_spec=pltpu.PrefetchScalarGridSpec(
            num_scalar_prefetch=2, grid=(P // tmg, N // tn),
            in_specs=[pl.BlockSpec((tmg, K), a_map), pl.BlockSpec((1, K, tn), w_map),
                      pl.BlockSpec((tmg, 1), lambda i, j, te, nt: (i, 0))],
            out_specs=pl.BlockSpec((tmg, tn), lambda i, j, te, nt: (i, j))),
        out_shape=jax.ShapeDtypeStruct((P, N), F32),
        compiler_params=_params(("arbitrary", "arbitrary"), blocks), name="moe_down",
    )(tile_expert, n_tiles, a, w2, row_w.reshape(P, 1))


def _combine_kernel(pos_ref, ys_hbm, x_ref, gt_ref, o_ref, buf_ref, sem, *, tt, db):
    base = pl.program_id(0) * tt

    def slot(r, k):
        return buf_ref.at[pl.ds(pl.multiple_of((k * tt + r) * db, db), db)]

    def issue(r, c):
        for k in range(TOP_K):
            pltpu.make_async_copy(ys_hbm.at[pos_ref[TOP_K * (base + r) + k]], slot(r, k), sem).start()
        return c

    lax.fori_loop(0, tt, issue, 0)

    def drain(r, c):
        for k in range(TOP_K):
            pltpu.make_async_copy(ys_hbm.at[0], slot(r, k), sem).wait()
        return c

    lax.fori_loop(0, tt, drain, 0)
    for cb in range(db):
        y = buf_ref[pl.ds(cb, tt, stride=db), :]
        for k in range(1, TOP_K):
            y = y + buf_ref[pl.ds(k * tt * db + cb, tt, stride=db), :]
        c0 = cb * LANES
        o_ref[:, c0:c0 + LANES] = x_ref[:, c0:c0 + LANES] + gt_ref[0, :, c0:c0 + LANES] * y


def _moe_combine(ys3, pos, x2, mod_l, gt_blk, S):
    T, D = x2.shape
    db = D // LANES
    tt = _tile(S, 128, SUBLANES)
    nps = S // tt
    blocks = [((tt, D), F32), ((1, 1, D), F32), ((tt, D), F32)]
    scratch = [((TOP_K * tt * db, LANES), F32)]
    return pl.pallas_call(
        functools.partial(_combine_kernel, tt=tt, db=db),
        grid_spec=pltpu.PrefetchScalarGridSpec(
            num_scalar_prefetch=1, grid=(T // tt,),
            in_specs=[pl.BlockSpec(memory_space=pl.ANY),
                      pl.BlockSpec((tt, D), lambda i, pos: (i, 0)),
                      pl.BlockSpec((1, 1, D), lambda i, pos: (i // nps, 0, gt_blk))],
            out_specs=pl.BlockSpec((tt, D), lambda i, pos: (i, 0)),
            scratch_shapes=[pltpu.VMEM((TOP_K * tt * db, LANES), F32), pltpu.SemaphoreType.DMA(())]),
        out_shape=jax.ShapeDtypeStruct((T, D), F32),
        compiler_params=_params(("arbitrary",), blocks, scratch), name="moe_combine",
    )(pos, ys3, x2, mod_l)


def _route_tables(top_i, top_w, n_experts, tmg, n_rows):
    T = top_i.shape[0]
    e_flat = top_i.reshape(-1)
    onehot = (e_flat[:, None] == jnp.arange(n_experts, dtype=I32)[None, :]).astype(I32)
    csum = jnp.cumsum(onehot, axis=0)
    rank = jnp.take_along_axis(csum, e_flat[:, None], axis=1)[:, 0] - 1
    counts = csum[-1]
    padded = ((counts + tmg - 1) // tmg) * tmg
    ends = jnp.cumsum(padded)
    pos = (ends - padded)[e_flat] + rank
    n_tiles = (ends[-1] // tmg).astype(I32).reshape(1)
    row_tok = jnp.zeros((n_rows,), I32).at[pos].set(jnp.arange(TOP_K * T, dtype=I32) // TOP_K)
    row_w = jnp.zeros((n_rows,), F32).at[pos].set(top_w.reshape(-1))
    tile_start = jnp.arange(n_rows // tmg, dtype=I32) * tmg
    tile_expert = jnp.searchsorted(ends, tile_start, side="right").astype(I32)
    last_expert = jnp.max(jnp.where(counts > 0, jnp.arange(n_experts, dtype=I32), 0))
    tile_expert = jnp.minimum(tile_expert, last_expert)
    return pos.astype(I32), row_tok, row_w, tile_expert, n_tiles


def _moe_ffn(x2, h, top_i, top_w, w1, w3, w2, mod_l, gt_blk, S):
    T, D = x2.shape
    E = w1.shape[0]
    tmg = _tile(S, 512, SUBLANES)
    n_rows = TOP_K * T + E * tmg
    pos, row_tok, row_w, tile_expert, n_tiles = _route_tables(top_i, top_w, E, tmg, n_rows)
    db = D // LANES
    hs = _gather_rows(h.reshape(T, db, LANES), row_tok, tmg).reshape(n_rows, D)
    act = _moe_swiglu(hs, w1, w3, tile_expert, n_tiles, tmg)
    ys = _moe_down(act, w2, row_w, tile_expert, n_tiles, tmg)
    return _moe_combine(ys.reshape(n_rows, db, LANES), pos, x2, mod_l, gt_blk, S)


def kernel(x, c, w_ada, b_ada, norm_mix, norm_ffn, w_in, w_pool, pool_scale, lb_logits, hg_norm, w_br_pool, w_br_sb,
           w_br_hg, w_out, ffn_w1, ffn_w3, ffn_w2, w_router, b_router, moe_w1, moe_w3, moe_w2, final_norm):
    B, S, D = x.shape
    depth = w_in.shape[0]
    pw = pool_scale.shape[1]
    sbw = w_br_sb.shape[1]
    hgw = w_br_hg.shape[1]
    c_q_sb = pw
    c_k_sb = c_q_sb + sbw
    c_v_sb = c_k_sb + sbw
    c_q_hg = c_v_sb + sbw
    c_f_hg = c_q_hg + hgw
    c_i_hg = c_f_hg + hgw
    c_g_hg = c_i_hg + hgw
    c_gate = c_g_hg + hgw

    x2 = x.reshape(B * S, D)
    mod = _ada_mod(c, w_ada.astype(BF16), b_ada)
    for l in range(depth):
        mod_l = mod[l].reshape(B, 1, N_MOD * D)
        h = _norm_mod(x2, norm_mix[l], mod_l, 1, 0, S)
        proj = _proj_matmul(h, w_in[l].astype(BF16), ACT_DTYPE)
        y_pool = _pool_mixer(proj, w_pool[l].astype(BF16), pool_scale[l], B, S)
        y_sb = _sb_attention(proj, c_q_sb, c_k_sb, c_v_sb, sbw // HEAD, B, S)
        y_hg = _hgrn2(proj, c_q_hg, c_f_hg, c_i_hg, c_g_hg, lb_logits, hg_norm[l], l, hgw // HEAD, B, S)
        merged = _merge_matmul(y_pool, y_sb, y_hg, w_br_pool[l].astype(BF16), w_br_sb[l].astype(BF16),
                               w_br_hg[l].astype(BF16), proj, c_gate)
        x2 = _resid_matmul(merged, w_out[l].astype(BF16), x2, mod_l, 2, S)
        j = l // 2
        if l % 2 == 0:
            h = _norm_mod(x2, norm_ffn[l], mod_l, 4, 3, S)
            act = _swiglu_matmul(h, ffn_w1[j].astype(BF16), ffn_w3[j].astype(BF16))
            x2 = _resid_matmul(act, ffn_w2[j].astype(BF16), x2, mod_l, 5, S, tk_pref=5632)
        else:
            h, idx, wgt = _norm_mod(x2, norm_ffn[l], mod_l, 4, 3, S, router=(w_router[j], b_router[j]))
            x2 = _moe_ffn(x2, h, idx[:, :TOP_K], wgt[:, :TOP_K], moe_w1[j].astype(BF16), moe_w3[j].astype(BF16),
                          moe_w2[j].astype(BF16), mod_l, 5, S)
    return _final_norm(x2, final_norm).reshape(B, S, D)
```

```python
import functools
import math

import jax
import jax.numpy as jnp
from jax import lax
from jax.experimental import pallas as pl
from jax.experimental.pallas import tpu as pltpu

F32 = jnp.float32
BF16 = jnp.bfloat16
I32 = jnp.int32

EPS = 1e-6
HEAD = 128
POOL_WINDOWS = (2, 4, 8, 16)
POOL_HALO = 16
HG_CHUNK = 64
HG_SUB = 16
N_MOD = 6
TOP_K = 2
LANES = 128
SUBLANES = 8
VMEM_CAP = 60000 * 1024
ACT_DTYPE = BF16


def _tile(n, pref, mult=LANES):
    if n <= pref:
        return n
    t = (pref // mult) * mult
    while t >= mult:
        if n % t == 0:
            return t
        t -= mult
    return n


def _nbytes(shape, dtype):
    n = 1
    for s in shape:
        n *= s
    return n * jnp.dtype(dtype).itemsize


def _params(sems, blocks, scratch=()):
    need = 2 * sum(_nbytes(s, d) for s, d in blocks) + sum(_nbytes(s, d) for s, d in scratch)
    limit = min(VMEM_CAP, max(32 * 1024 * 1024, need + need // 4 + (4 << 20)))
    return pltpu.CompilerParams(dimension_semantics=sems, vmem_limit_bytes=int(limit))


def _split3(x):
    hi = x.astype(BF16)
    r1 = x - hi.astype(F32)
    mid = r1.astype(BF16)
    lo = (r1 - mid.astype(F32)).astype(BF16)
    return hi, mid, lo


def _dot(a, b):
    return jnp.dot(a, b, preferred_element_type=F32)


def _ada_kernel(c_ref, w_ref, b_ref, o_ref):
    ca = jax.nn.silu(c_ref[...])
    o_ref[0] = _dot(ca.astype(BF16), w_ref[0].astype(BF16)) + b_ref[0]


def _ada_mod(c, w_ada, b_ada):
    L, D, N = w_ada.shape
    B = c.shape[0]
    tn = _tile(N, 1024)
    blocks = [((B, D), F32), ((1, D, tn), w_ada.dtype), ((1, 1, tn), F32), ((1, B, tn), F32), ((D, tn), BF16)]
    return pl.pallas_call(
        _ada_kernel,
        grid=(L, N // tn),
        in_specs=[pl.BlockSpec((B, D), lambda l, j: (0, 0)),
                  pl.BlockSpec((1, D, tn), lambda l, j: (l, 0, j)),
                  pl.BlockSpec((1, 1, tn), lambda l, j: (l, 0, j))],
        out_specs=pl.BlockSpec((1, B, tn), lambda l, j: (l, 0, j)),
        out_shape=jax.ShapeDtypeStruct((L, B, N), F32),
        compiler_params=_params(("parallel", "parallel"), blocks),
        name="ada_mod",
    )(c, w_ada, b_ada.reshape(L, 1, N))


def _norm_mod_value(x_ref, g_ref, sc_ref, sh_ref):
    x = x_ref[...]
    ms = jnp.mean(x * x, axis=-1, keepdims=True)
    y = x * lax.rsqrt(ms + EPS) * g_ref[...]
    return y * (1.0 + sc_ref[0]) + sh_ref[0]


def _norm_mod_kernel(x_ref, g_ref, sc_ref, sh_ref, o_ref):
    o_ref[...] = _norm_mod_value(x_ref, g_ref, sc_ref, sh_ref).astype(o_ref.dtype)


def _norm_mod_router_kernel(x_ref, g_ref, sc_ref, sh_ref, wr_ref, br_ref, o_ref, idx_ref, wgt_ref, slab_ref, *,
                            n_experts):
    h = _norm_mod_value(x_ref, g_ref, sc_ref, sh_ref)
    _rows_to_slabs(h, slab_ref, o_ref)
    h_hi = h.astype(BF16)
    h_lo = (h - h_hi.astype(F32)).astype(BF16)
    logits = (_dot(h_hi, wr_ref[0]) + _dot(h_hi, wr_ref[1]) + _dot(h_lo, wr_ref[0])
              + _dot(h_hi, wr_ref[2]) + _dot(h_lo, wr_ref[1])) + br_ref[...]
    lane = lax.broadcasted_iota(I32, logits.shape, 1)
    lanef = lane.astype(F32)
    lg = jnp.where(lane < n_experts, logits, -jnp.inf)
    m1 = jnp.max(lg, axis=-1, keepdims=True)
    i1 = jnp.min(jnp.where(lg == m1, lanef, float(LANES)), axis=-1, keepdims=True)
    lg2 = jnp.where(lanef == i1, -jnp.inf, lg)
    m2 = jnp.max(lg2, axis=-1, keepdims=True)
    i2 = jnp.min(jnp.where(lg2 == m2, lanef, float(LANES)), axis=-1, keepdims=True)
    e2 = jnp.exp(m2 - m1)
    w1 = 1.0 / (1.0 + e2)
    w2 = e2 / (1.0 + e2)
    idx_ref[...] = jnp.where(lane == 0, i1, jnp.where(lane == 1, i2, 0.0)).astype(I32)
    wgt_ref[...] = jnp.where(lane == 0, w1, jnp.where(lane == 1, w2, 0.0))


def _norm_mod(x2, gain, mod_l, sc_blk, sh_blk, S, router=None):
    T, D = x2.shape
    tm = _tile(S, 256, SUBLANES)
    nps = S // tm
    in_specs = [pl.BlockSpec((tm, D), lambda i: (i, 0)),
                pl.BlockSpec((1, D), lambda i: (0, 0)),
                pl.BlockSpec((1, 1, D), lambda i: (i // nps, 0, sc_blk)),
                pl.BlockSpec((1, 1, D), lambda i: (i // nps, 0, sh_blk))]
    blocks = [((tm, D), F32), ((1, D), F32), ((1, 1, D), F32), ((1, 1, D), F32), ((tm, D), ACT_DTYPE)]
    h_spec = pl.BlockSpec((tm, D), lambda i: (i, 0))
    h_shape = jax.ShapeDtypeStruct((T, D), ACT_DTYPE)
    if router is None:
        return pl.pallas_call(
            _norm_mod_kernel, grid=(T // tm,), in_specs=in_specs, out_specs=h_spec, out_shape=h_shape,
            compiler_params=_params(("parallel",), blocks), name="norm_mod",
        )(x2, gain.reshape(1, D), mod_l, mod_l)
    w_router, b_router = router
    E = w_router.shape[1]
    db = _slab_pitch(D // LANES)
    wr = jnp.zeros((D, LANES), F32).at[:, :E].set(w_router)
    wr3 = jnp.stack(_split3(wr))
    br = jnp.zeros((1, LANES), F32).at[0, :E].set(b_router)
    in_specs += [pl.BlockSpec((3, D, LANES), lambda i: (0, 0, 0)), pl.BlockSpec((1, LANES), lambda i: (0, 0))]
    blocks += [((3, D, LANES), BF16), ((tm, LANES), F32), ((tm, LANES), F32), ((tm, D), F32)]
    scratch = [((tm * db, LANES), F32)]
    lane_spec = pl.BlockSpec((tm, LANES), lambda i: (i, 0))
    return pl.pallas_call(
        functools.partial(_norm_mod_router_kernel, n_experts=E),
        grid=(T // tm,), in_specs=in_specs,
        out_specs=[pl.BlockSpec((tm, db, LANES), lambda i: (i, 0, 0)), lane_spec, lane_spec],
        out_shape=[jax.ShapeDtypeStruct((T, db, LANES), F32), jax.ShapeDtypeStruct((T, LANES), I32),
                   jax.ShapeDtypeStruct((T, LANES), F32)],
        scratch_shapes=[pltpu.VMEM((tm * db, LANES), F32)],
        compiler_params=_params(("parallel",), blocks, scratch), name="norm_mod_router",
    )(x2, gain.reshape(1, D), mod_l, mod_l, wr3, br)


def _final_norm_kernel(x_ref, g_ref, o_ref):
    x = x_ref[...]
    ms = jnp.mean(x * x, axis=-1, keepdims=True)
    o_ref[...] = x * lax.rsqrt(ms + EPS) * g_ref[...]


def _final_norm(x2, gain):
    T, D = x2.shape
    tm = _tile(T, 256, SUBLANES)
    blocks = [((tm, D), F32), ((1, D), F32), ((tm, D), F32)]
    return pl.pallas_call(
        _final_norm_kernel, grid=(T // tm,),
        in_specs=[pl.BlockSpec((tm, D), lambda i: (i, 0)), pl.BlockSpec((1, D), lambda i: (0, 0))],
        out_specs=pl.BlockSpec((tm, D), lambda i: (i, 0)),
        out_shape=jax.ShapeDtypeStruct((T, D), F32),
        compiler_params=_params(("parallel",), blocks), name="final_norm",
    )(x2, gain.reshape(1, D))


def _proj_kernel(a_ref, w_ref, o_ref):
    o_ref[...] = _dot(a_ref[...], w_ref[0].astype(BF16)).astype(o_ref.dtype)


def _proj_matmul(a, w, l, out_dtype):
    M, K = a.shape
    N = w.shape[2]
    tm, tn = _tile(M, 2048), _tile(N, 512)
    blocks = [((K, tn), w.dtype), ((tm, tn), out_dtype)]
    scratch = [((tm, K), a.dtype), ((K, tn), BF16)]
    return pl.pallas_call(
        _proj_kernel, grid=(M // tm, N // tn),
        in_specs=[pl.BlockSpec((tm, K), lambda i, j: (i, 0), pipeline_mode=pl.Buffered(1)),
                  pl.BlockSpec((1, K, tn), lambda i, j: (l, 0, j))],
        out_specs=pl.BlockSpec((tm, tn), lambda i, j: (i, j)),
        out_shape=jax.ShapeDtypeStruct((M, N), out_dtype),
        compiler_params=_params(("parallel", "arbitrary"), blocks, scratch), name="proj_matmul",
    )(a, w)


def _swiglu_kernel(a_ref, w1_ref, w3_ref, o_ref):
    a = a_ref[...]
    up = _dot(a, w3_ref[0].astype(BF16))
    o_ref[...] = (jax.nn.silu(_dot(a, w1_ref[0].astype(BF16))) * up).astype(o_ref.dtype)


def _swiglu_matmul(a, w1, w3, l):
    M, K = a.shape
    N = w1.shape[2]
    tm, tn = _tile(M, 2048), _tile(N, 256)
    blocks = [((K, tn), w1.dtype), ((K, tn), w3.dtype), ((tm, tn), ACT_DTYPE)]
    scratch = [((tm, K), a.dtype), ((K, tn), BF16), ((K, tn), BF16)]
    return pl.pallas_call(
        _swiglu_kernel, grid=(M // tm, N // tn),
        in_specs=[pl.BlockSpec((tm, K), lambda i, j: (i, 0), pipeline_mode=pl.Buffered(1)),
                  pl.BlockSpec((1, K, tn), lambda i, j: (l, 0, j)),
                  pl.BlockSpec((1, K, tn), lambda i, j: (l, 0, j))],
        out_specs=pl.BlockSpec((tm, tn), lambda i, j: (i, j)),
        out_shape=jax.ShapeDtypeStruct((M, N), ACT_DTYPE),
        compiler_params=_params(("parallel", "arbitrary"), blocks, scratch), name="swiglu_matmul",
    )(a, w1, w3)


def _resid_kernel(a_ref, w_ref, x_ref, gt_ref, o_ref, acc_ref, *, nk):
    part = _dot(a_ref[...], w_ref[0].astype(BF16))
    if nk == 1:
        o_ref[...] = x_ref[...] + gt_ref[0] * part
        return
    k = pl.program_id(2)

    @pl.when(k == 0)
    def _():
        acc_ref[...] = part

    @pl.when(jnp.logical_and(k > 0, k < nk - 1))
    def _():
        acc_ref[...] += part

    @pl.when(k == nk - 1)
    def _():
        o_ref[...] = x_ref[...] + gt_ref[0] * (acc_ref[...] + part)


def _resid_matmul(a, w, l, x2, mod_l, gt_blk, S, tk_pref=4096):
    M, K = a.shape
    N = w.shape[2]
    tk = _tile(K, tk_pref)
    nk = K // tk
    tm, tn = _tile(S, 1024), _tile(N, 512)
    nps = S // tm
    gpb = N // tn
    blocks = [((tm, tk), a.dtype), ((tk, tn), w.dtype), ((tm, tn), F32), ((1, 1, tn), F32), ((tm, tn), F32)]
    scratch = [((tm, tn), F32), ((tk, tn), BF16)]
    return pl.pallas_call(
        functools.partial(_resid_kernel, nk=nk), grid=(M // tm, N // tn, nk),
        in_specs=[pl.BlockSpec((tm, tk), lambda i, j, k: (i, k)),
                  pl.BlockSpec((1, tk, tn), lambda i, j, k: (l, k, j)),
                  pl.BlockSpec((tm, tn), lambda i, j, k: (i, j)),
                  pl.BlockSpec((1, 1, tn), lambda i, j, k: (i // nps, 0, gt_blk * gpb + j))],
        out_specs=pl.BlockSpec((tm, tn), lambda i, j, k: (i, j)),
        out_shape=jax.ShapeDtypeStruct((M, N), F32),
        scratch_shapes=[pltpu.VMEM((tm, tn), F32)],
        compiler_params=_params(("parallel", "arbitrary", "arbitrary"), blocks, scratch), name="resid_matmul",
    )(a, w, x2, mod_l)


def _merge_kernel(yp_ref, ys_ref, yh_ref, wp_ref, ws_ref, wh_ref, g0_ref, g1_ref, g2_ref, o_ref):
    def gate(g_ref):
        return jax.nn.sigmoid(g_ref[...].astype(F32))
    m = (gate(g0_ref) * _dot(yp_ref[...], wp_ref[0].astype(BF16))
         + gate(g1_ref) * _dot(ys_ref[...], ws_ref[0].astype(BF16))
         + gate(g2_ref) * _dot(yh_ref[...], wh_ref[0].astype(BF16)))
    o_ref[...] = m.astype(o_ref.dtype)


def _merge_matmul(y_pool, y_sb, y_hg, w_p, w_s, w_h, l, proj, gate_col0):
    M = y_pool.shape[0]
    D = w_p.shape[2]
    tm, tn = _tile(M, 1024), _tile(math.gcd(D, gate_col0), 512)
    g0 = gate_col0 // tn
    gpb = D // tn
    ks = (y_pool.shape[1], y_sb.shape[1], y_hg.shape[1])
    blocks = ([((tm, k), ACT_DTYPE) for k in ks] + [((k, tn), w.dtype) for k, w in zip(ks, (w_p, w_s, w_h))]
              + [((tm, tn), proj.dtype)] * 3 + [((tm, tn), ACT_DTYPE)])
    a_specs = [pl.BlockSpec((tm, k), lambda i, j: (i, 0)) for k in ks]
    w_specs = [pl.BlockSpec((1, k, tn), lambda i, j: (l, 0, j)) for k in ks]
    g_specs = [pl.BlockSpec((tm, tn), functools.partial(lambda i, j, b: (i, g0 + b * gpb + j), b=b)) for b in range(3)]
    return pl.pallas_call(
        _merge_kernel, grid=(M // tm, D // tn),
        in_specs=a_specs + w_specs + g_specs,
        out_specs=pl.BlockSpec((tm, tn), lambda i, j: (i, j)),
        out_shape=jax.ShapeDtypeStruct((M, D), ACT_DTYPE),
        compiler_params=_params(("parallel", "arbitrary"), blocks), name="merge_matmul",
    )(y_pool, y_sb, y_hg, w_p, w_s, w_h, proj, proj, proj)


def _pool_kernel(u_ref, wp_ref, ps_ref, o_ref, ext_ref, *, ts, pc):
    si = pl.program_id(1)

    @pl.when(si == 0)
    def _():
        ext_ref[0:POOL_HALO, :] = jnp.zeros((POOL_HALO, ext_ref.shape[1]), F32)

    u = u_ref[...].astype(F32)
    ext_ref[POOL_HALO:POOL_HALO + ts, :] = u
    pos = si * ts + lax.broadcasted_iota(I32, (ts, 1), 0)
    for g, w in enumerate(POOL_WINDOWS):
        c0, c1 = g * pc, (g + 1) * pc
        ug = u[:, c0:c1]
        acc = ug
        for j in range(1, w):
            acc = acc + ext_ref[POOL_HALO - j:POOL_HALO - j + ts, c0:c1]
        cnt = jnp.minimum(pos + 1, w).astype(F32)
        pooled = acc / cnt - ug
        y = _dot(pooled.astype(BF16), wp_ref[g]) * ps_ref[:, c0:c1]
        o_ref[:, c0:c1] = y.astype(o_ref.dtype)
    ext_ref[0:POOL_HALO, :] = ext_ref[ts:ts + POOL_HALO, :]


def _pool_mixer(proj, w_pool, pool_scale, B, S):
    G, pc, _ = w_pool.shape
    pw = G * pc
    ts = _tile(S, 512, POOL_HALO)
    ns = S // ts
    blocks = [((ts, pw), proj.dtype), ((G, pc, pc), BF16), ((1, pw), F32), ((ts, pw), ACT_DTYPE)]
    scratch = [((ts + POOL_HALO, pw), F32)]
    return pl.pallas_call(
        functools.partial(_pool_kernel, ts=ts, pc=pc), grid=(B, ns),
        in_specs=[pl.BlockSpec((ts, pw), lambda b, s: (b * ns + s, 0)),
                  pl.BlockSpec((G, pc, pc), lambda b, s: (0, 0, 0)),
                  pl.BlockSpec((1, pw), lambda b, s: (0, 0))],
        out_specs=pl.BlockSpec((ts, pw), lambda b, s: (b * ns + s, 0)),
        out_shape=jax.ShapeDtypeStruct((B * S, pw), ACT_DTYPE),
        scratch_shapes=[pltpu.VMEM((ts + POOL_HALO, pw), F32)],
        compiler_params=_params(("parallel", "arbitrary"), blocks, scratch), name="pool_mixer",
    )(proj, w_pool, pool_scale.reshape(1, pw))


def _sb_kernel(q_ref, k_ref, v_ref, o_ref, *, tq, hb):
    qi = pl.program_id(2)
    scale = HEAD ** -0.5
    rows = lax.broadcasted_iota(I32, (tq, tq), 0)
    cols = lax.broadcasted_iota(I32, (tq, tq), 1)
    later = (rows > cols).astype(BF16)
    below = cols < rows
    qs = [q_ref[:, h * HEAD:(h + 1) * HEAD].astype(BF16) for h in range(hb)]

    def tile(h, kj, acc, run, diagonal):
        k0 = pl.multiple_of(kj * tq, tq)
        ks = k_ref[pl.ds(k0, tq), h * HEAD:(h + 1) * HEAD].astype(BF16)
        vs = v_ref[pl.ds(k0, tq), h * HEAD:(h + 1) * HEAD].astype(BF16)
        z = lax.dot_general(qs[h], ks, (((1,), (1,)), ((), ())), preferred_element_type=F32) * scale
        log_beta = jnp.minimum(z, 0.0) - jnp.log1p(jnp.exp(-jnp.abs(z)))
        log_keep = log_beta - z
        if diagonal:
            log_keep = jnp.where(below, log_keep, 0.0)
        hi = log_keep.astype(BF16)
        lo = (log_keep - hi.astype(F32)).astype(BF16)
        after = _dot(hi, later) + _dot(lo, later) + run
        a = jnp.exp(log_beta + after)
        if diagonal:
            a = jnp.where(below, a, 0.0)
        acc = acc + _dot(a.astype(BF16), vs)
        run = run + jnp.sum(log_keep, axis=1, keepdims=True)
        return acc, run

    carry = []
    for h in range(hb):
        carry += list(tile(h, qi, jnp.zeros((tq, HEAD), F32), jnp.zeros((tq, 1), F32), True))

    def step(jj, carry):
        out = []
        for h in range(hb):
            out += list(tile(h, qi - jj, carry[2 * h], carry[2 * h + 1], False))
        return tuple(out)

    carry = lax.fori_loop(1, qi + 1, step, tuple(carry))
    for h in range(hb):
        o_ref[:, h * HEAD:(h + 1) * HEAD] = carry[2 * h].astype(o_ref.dtype)


def _heads_per_step(n_heads, col0s, prefs):
    for hb in prefs:
        if n_heads % hb == 0 and all(c % (hb * HEAD) == 0 for c in col0s):
            return hb
    return 1


def _sb_attention(proj, q_col0, k_col0, v_col0, n_heads, B, S):
    tq = _tile(S, 256)
    nq = S // tq
    hb = _heads_per_step(n_heads, (q_col0, k_col0, v_col0), (2,))
    w = hb * HEAD
    qb, kb, vb = q_col0 // w, k_col0 // w, v_col0 // w
    blocks = [((tq, w), proj.dtype), ((S, w), proj.dtype), ((S, w), proj.dtype), ((tq, w), ACT_DTYPE)]
    return pl.pallas_call(
        functools.partial(_sb_kernel, tq=tq, hb=hb), grid=(B, n_heads // hb, nq),
        in_specs=[pl.BlockSpec((tq, w), lambda b, h, i: (b * nq + i, qb + h)),
                  pl.BlockSpec((S, w), lambda b, h, i: (b, kb + h)),
                  pl.BlockSpec((S, w), lambda b, h, i: (b, vb + h))],
        out_specs=pl.BlockSpec((tq, w), lambda b, h, i: (b * nq + i, h)),
        out_shape=jax.ShapeDtypeStruct((B * S, n_heads * HEAD), ACT_DTYPE),
        compiler_params=_params(("parallel", "parallel", "arbitrary"), blocks), name="sb_attention",
    )(proj, proj, proj)


def _hg_kernel(q_ref, f_ref, i_ref, g_ref, lbl_ref, gn_ref, o_ref, st_ref, b_ref, kk_ref, *, layer, n_chunks, hb):
    C = HG_CHUNK
    si = pl.program_id(2)

    @pl.when(si == 0)
    def _():
        st_ref[...] = jnp.zeros(st_ref.shape, F32)

    lbl = lbl_ref[...]
    e = jnp.exp(lbl - jnp.max(lbl, axis=0, keepdims=True))
    p = e / jnp.sum(e, axis=0, keepdims=True)
    lb = jnp.zeros((1, hb * HEAD), F32)
    for j in range(1, layer + 1):
        lb = lb + p[j:j + 1, :]
    gn = gn_ref[...]
    rows = lax.broadcasted_iota(I32, (C, C), 0)
    cols = lax.broadcasted_iota(I32, (C, C), 1)
    incl = (rows >= cols).astype(BF16)
    lane = lax.broadcasted_iota(I32, (SUBLANES, C), 1)
    sub = lax.broadcasted_iota(I32, (SUBLANES, HEAD), 0)
    nb = C // HG_SUB
    pairs = [(bi, bj) for bi in range(nb) for bj in range(bi)]

    def head_chunk(h, r0):
        hs = slice(h * HEAD, (h + 1) * HEAD)
        qh = q_ref[pl.ds(r0, C), hs].astype(F32)
        fh = f_ref[pl.ds(r0, C), hs].astype(F32)
        ih = i_ref[pl.ds(r0, C), hs].astype(BF16)
        gh = g_ref[pl.ds(r0, C), hs].astype(F32)
        lbh = lb[:, hs]
        f = lbh + (1.0 - lbh) * jax.nn.sigmoid(fh)
        kk = 1.0 - f
        qa = jax.nn.silu(qh)
        hi, mid, lo = _split3(jnp.log(f))
        b = _dot(incl, hi) + _dot(incl, mid) + _dot(incl, lo)
        b_ref[h] = b
        kk_ref[h] = kk
        edge = [jnp.zeros((1, HEAD), F32)] + [b[m * HG_SUB - 1:m * HG_SUB, :] for m in range(1, nb + 1)]
        b_start = jnp.concatenate([jnp.broadcast_to(edge[m], (HG_SUB, HEAD)) for m in range(nb)], axis=0)
        b_end = jnp.concatenate([jnp.broadcast_to(edge[m + 1], (HG_SUB, HEAD)) for m in range(nb)], axis=0)
        qd_sub = qa * jnp.exp(b - b_start)
        kd_sub = kk * jnp.exp(b_end - b)
        lhs = jnp.concatenate([qd_sub[bi * HG_SUB:(bi + 1) * HG_SUB, :] * jnp.exp(edge[bi] - edge[bj + 1])
                               for bi, bj in pairs], axis=0)
        cross = lax.dot_general(lhs.astype(BF16), kd_sub.astype(BF16), (((1,), (1,)), ((), ())),
                                preferred_element_type=F32)
        pieces = []
        for g in range(C // SUBLANES):
            t0 = g * SUBLANES
            bi = t0 // HG_SUB
            bt = b[t0:t0 + SUBLANES, :]
            qt = qa[t0:t0 + SUBLANES, :]
            sc = jnp.zeros((SUBLANES, C), F32)
            for p, (pi, pj) in enumerate(pairs):
                if pi == bi:
                    r = p * HG_SUB + (t0 - bi * HG_SUB)
                    in_block = jnp.logical_and(lane >= pj * HG_SUB, lane < (pj + 1) * HG_SUB)
                    sc = jnp.where(in_block, cross[r:r + SUBLANES, :], sc)
            for s in range(bi * HG_SUB, t0 + SUBLANES):
                d = bt - b_ref[h, pl.ds(s, 1), :]
                if s >= t0:
                    d = jnp.where(sub + t0 >= s, d, -jnp.inf)
                wgt = qt * kk_ref[h, pl.ds(s, 1), :] * jnp.exp(d)
                col = jnp.sum(wgt, axis=1, keepdims=True)
                sc = jnp.where(lane == s, col, sc)
            pieces.append(sc)
        scores = jnp.concatenate(pieces, axis=0)
        st = st_ref[h]
        o = _dot(scores.astype(BF16), ih)
        qd = (qd_sub * jnp.exp(b_start)).astype(BF16)
        o = o + lax.dot_general(qd, st.astype(BF16), (((1,), (1,)), ((), ())), preferred_element_type=F32)
        b_last = edge[nb]
        kd = (kd_sub * jnp.exp(b_last - b_end)).astype(BF16)
        st_ref[h] = st * jnp.exp(b_last) + lax.dot_general(ih, kd, (((0,), (0,)), ((), ())),
                                                           preferred_element_type=F32)
        y = o * lax.rsqrt(jnp.mean(o * o, axis=-1, keepdims=True) + EPS) * gn
        o_ref[pl.ds(r0, C), hs] = (y * jax.nn.silu(gh)).astype(o_ref.dtype)

    def chunk(ci, carry):
        r0 = pl.multiple_of(ci * C, C)
        for h in range(hb):
            head_chunk(h, r0)
        return carry

    lax.fori_loop(0, n_chunks, chunk, 0)


def _hgrn2(proj, q_col0, f_col0, i_col0, g_col0, lb_logits, hg_norm, layer, n_heads, B, S):
    ts = _tile(S, 512, HG_CHUNK)
    ns = S // ts
    L = lb_logits.shape[0]
    hb = _heads_per_step(n_heads, (q_col0, f_col0, i_col0, g_col0), (4, 2))
    w = hb * HEAD
    qb, fb, ib, gb = q_col0 // w, f_col0 // w, i_col0 // w, g_col0 // w

    def col_spec(cb):
        return pl.BlockSpec((ts, w), lambda b, h, s: (b * ns + s, cb + h))

    blocks = [((ts, w), proj.dtype)] * 4 + [((L, w), F32), ((1, HEAD), F32), ((ts, w), ACT_DTYPE)]
    scratch = [((hb, HEAD, HEAD), F32), ((hb, HG_CHUNK, HEAD), F32), ((hb, HG_CHUNK, HEAD), F32)]
    return pl.pallas_call(
        functools.partial(_hg_kernel, layer=layer, n_chunks=ts // HG_CHUNK, hb=hb), grid=(B, n_heads // hb, ns),
        in_specs=[col_spec(qb), col_spec(fb), col_spec(ib), col_spec(gb),
                  pl.BlockSpec((L, w), lambda b, h, s: (0, h)),
                  pl.BlockSpec((1, HEAD), lambda b, h, s: (0, 0))],
        out_specs=pl.BlockSpec((ts, w), lambda b, h, s: (b * ns + s, h)),
        out_shape=jax.ShapeDtypeStruct((B * S, n_heads * HEAD), ACT_DTYPE),
        scratch_shapes=[pltpu.VMEM(s, d) for s, d in scratch],
        compiler_params=_params(("parallel", "parallel", "arbitrary"), blocks, scratch), name="hgrn2",
    )(proj, proj, proj, proj, lb_logits, hg_norm.reshape(1, HEAD))


def _slab_pitch(db):
    return db if (db // SUBLANES) % 2 == 1 else db + SUBLANES


def _moe_swiglu_kernel(te_ref, nt_ref, tok_ref, h_hbm, w1_ref, w3_ref, o_ref, buf_ref, a_ref, sem, *, tmg, db):
    i = pl.program_id(0)
    j = pl.program_id(1)
    nt = nt_ref[0]
    pitch = _slab_pitch(db)

    def row_copy(tile, r, slot):
        dst = buf_ref.at[slot, pl.ds(pl.multiple_of(r * pitch, SUBLANES), db)]
        return pltpu.make_async_copy(h_hbm.at[tok_ref[tile * tmg + r], pl.ds(0, db)], dst, sem.at[slot])

    def start_gather(tile, slot):
        lax.fori_loop(0, tmg, lambda r, c: (row_copy(tile, r, slot).start(), c)[1], 0)

    def wait_gather(tile, slot):
        lax.fori_loop(0, tmg, lambda r, c: (row_copy(tile, r, slot).wait(), c)[1], 0)

    @pl.when(j == 0)
    def _():
        slot = lax.rem(i, 2)

        @pl.when(i == 0)
        def _():
            start_gather(0, 0)

        @pl.when(i < nt)
        def _():
            wait_gather(i, slot)
            for cb in range(db):
                a_ref[:, cb * LANES:(cb + 1) * LANES] = buf_ref[slot, pl.ds(cb, tmg, stride=pitch), :].astype(BF16)

        @pl.when(i + 1 < nt)
        def _():
            start_gather(i + 1, 1 - slot)

    @pl.when(i < nt)
    def _():
        a = a_ref[...]
        o_ref[...] = (jax.nn.silu(_dot(a, w1_ref[0])) * _dot(a, w3_ref[0])).astype(o_ref.dtype)

    @pl.when(i >= nt)
    def _():
        o_ref[...] = jnp.zeros(o_ref.shape, o_ref.dtype)


def _moe_swiglu(h3, row_tok, w1, w3, tile_expert, n_tiles, tmg):
    P = row_tok.shape[0]
    K = w1.shape[1]
    db = K // LANES
    N = w1.shape[2]
    tn = _tile(N, 512)
    blocks = [((1, K, tn), BF16), ((1, K, tn), BF16), ((tmg, tn), ACT_DTYPE)]
    buf_shape = (2, tmg * _slab_pitch(db), LANES)
    scratch = [(buf_shape, F32), ((tmg, K), BF16)]

    def w_map(i, j, te, nt, tok):
        return (te[i], 0, jnp.where(i < nt[0], j, N // tn - 1))

    return pl.pallas_call(
        functools.partial(_moe_swiglu_kernel, tmg=tmg, db=db),
        grid_spec=pltpu.PrefetchScalarGridSpec(
            num_scalar_prefetch=3, grid=(P // tmg, N // tn),
            in_specs=[pl.BlockSpec(memory_space=pl.ANY), pl.BlockSpec((1, K, tn), w_map), pl.BlockSpec((1, K, tn), w_map)],
            out_specs=pl.BlockSpec((tmg, tn), lambda i, j, te, nt, tok: (i, j)),
            scratch_shapes=[pltpu.VMEM(buf_shape, F32), pltpu.VMEM((tmg, K), BF16),
                            pltpu.SemaphoreType.DMA((2,))]),
        out_shape=jax.ShapeDtypeStruct((P, N), ACT_DTYPE),
        compiler_params=_params(("arbitrary", "arbitrary"), blocks, scratch), name="moe_swiglu",
    )(tile_expert, n_tiles, row_tok, h3, w1, w3)


def _rows_to_slabs(val, slab_ref, o_ref):
    m, n = o_ref.shape[0], o_ref.shape[1]
    nv = val.shape[1] // LANES
    for c in range(nv):
        slab_ref[pl.ds(c, m, stride=n), :] = val[:, c * LANES:(c + 1) * LANES]
    for c in range(nv, n):
        slab_ref[pl.ds(c, m, stride=n), :] = jnp.zeros((m, LANES), val.dtype)
    o_ref[...] = slab_ref[...].reshape(m, n, LANES)


def _moe_down_kernel(te_ref, nt_ref, a_ref, w_ref, rw_ref, o_ref, slab_ref):
    active = pl.program_id(0) < nt_ref[0]

    @pl.when(active)
    def _():
        _rows_to_slabs(_dot(a_ref[...], w_ref[0]) * rw_ref[...], slab_ref, o_ref)

    @pl.when(jnp.logical_not(active))
    def _():
        o_ref[...] = jnp.zeros(o_ref.shape, o_ref.dtype)


def _moe_down(a, w2, row_w, tile_expert, n_tiles, tmg):
    P, K = a.shape
    N = w2.shape[2]
    tn = _tile(N, 1024)
    nb = tn // LANES
    blocks = [((tmg, K), a.dtype), ((1, K, tn), BF16), ((tmg, LANES), F32), ((tmg, tn), F32)]
    scratch = [((tmg * nb, LANES), F32)]

    def a_map(i, j, te, nt):
        return (jnp.minimum(i, nt[0] - 1), 0)

    def w_map(i, j, te, nt):
        return (te[i], 0, jnp.where(i < nt[0], j, N // tn - 1))

    return pl.pallas_call(
        _moe_down_kernel,
        grid_spec=pltpu.PrefetchScalarGridSpec(
            num_scalar_prefetch=2, grid=(P // tmg, N // tn),
            in_specs=[pl.BlockSpec((tmg, K), a_map), pl.BlockSpec((1, K, tn), w_map),
                      pl.BlockSpec((tmg, 1), lambda i, j, te, nt: (i, 0))],
            out_specs=pl.BlockSpec((tmg, nb, LANES), lambda i, j, te, nt: (i, j, 0)),
            scratch_shapes=[pltpu.VMEM((tmg * nb, LANES), F32)]),
        out_shape=jax.ShapeDtypeStruct((P, N // LANES, LANES), F32),
        compiler_params=_params(("arbitrary", "arbitrary"), blocks, scratch), name="moe_down",
    )(tile_expert, n_tiles, a, w2, row_w.reshape(P, 1))


def _combine_kernel(pos_ref, ys_hbm, x_ref, gt_ref, o_ref, buf_ref, sem, *, tt, db):
    i = pl.program_id(0)
    pitch = _slab_pitch(db)

    def row_copy(tile, r, k, slot):
        dst = buf_ref.at[slot, pl.ds(pl.multiple_of((k * tt + r) * pitch, SUBLANES), db)]
        return pltpu.make_async_copy(ys_hbm.at[pos_ref[TOP_K * (tile * tt + r) + k]], dst, sem.at[slot])

    def for_rows(fn):
        def body(r, c):
            for k in range(TOP_K):
                fn(r, k)
            return c
        lax.fori_loop(0, tt, body, 0)

    slot = lax.rem(i, 2)

    @pl.when(i == 0)
    def _():
        for_rows(lambda r, k: row_copy(0, r, k, 0).start())

    for_rows(lambda r, k: row_copy(i, r, k, slot).wait())

    @pl.when(i + 1 < pl.num_programs(0))
    def _():
        for_rows(lambda r, k: row_copy(i + 1, r, k, 1 - slot).start())

    for cb in range(db):
        y = buf_ref[slot, pl.ds(cb, tt, stride=pitch), :]
        for k in range(1, TOP_K):
            y = y + buf_ref[slot, pl.ds(k * tt * pitch + cb, tt, stride=pitch), :]
        c0 = cb * LANES
        o_ref[:, c0:c0 + LANES] = x_ref[:, c0:c0 + LANES] + gt_ref[0, :, c0:c0 + LANES] * y


def _moe_combine(ys3, pos, x2, mod_l, gt_blk, S):
    T, D = x2.shape
    db = D // LANES
    tt = _tile(S, 256, SUBLANES)
    nps = S // tt
    blocks = [((tt, D), F32), ((1, 1, D), F32), ((tt, D), F32)]
    buf_shape = (2, TOP_K * tt * _slab_pitch(db), LANES)
    scratch = [(buf_shape, F32)]
    return pl.pallas_call(
        functools.partial(_combine_kernel, tt=tt, db=db),
        grid_spec=pltpu.PrefetchScalarGridSpec(
            num_scalar_prefetch=1, grid=(T // tt,),
            in_specs=[pl.BlockSpec(memory_space=pl.ANY),
                      pl.BlockSpec((tt, D), lambda i, pos: (i, 0)),
                      pl.BlockSpec((1, 1, D), lambda i, pos: (i // nps, 0, gt_blk))],
            out_specs=pl.BlockSpec((tt, D), lambda i, pos: (i, 0)),
            scratch_shapes=[pltpu.VMEM(buf_shape, F32), pltpu.SemaphoreType.DMA((2,))]),
        out_shape=jax.ShapeDtypeStruct((T, D), F32),
        compiler_params=_params(("arbitrary",), blocks, scratch), name="moe_combine",
    )(pos, ys3, x2, mod_l)


def _route_tables(top_i, top_w, n_experts, tmg, n_rows):
    T = top_i.shape[0]
    e_flat = top_i.reshape(-1)
    onehot = (e_flat[:, None] == jnp.arange(n_experts, dtype=I32)[None, :]).astype(I32)
    csum = jnp.cumsum(onehot, axis=0)
    rank = jnp.take_along_axis(csum, e_flat[:, None], axis=1)[:, 0] - 1
    counts = csum[-1]
    padded = ((counts + tmg - 1) // tmg) * tmg
    ends = jnp.cumsum(padded)
    pos = (ends - padded)[e_flat] + rank
    n_tiles = (ends[-1] // tmg).astype(I32).reshape(1)
    row_tok = jnp.zeros((n_rows,), I32).at[pos].set(jnp.arange(TOP_K * T, dtype=I32) // TOP_K)
    row_w = jnp.zeros((n_rows,), F32).at[pos].set(top_w.reshape(-1))
    tile_start = jnp.arange(n_rows // tmg, dtype=I32) * tmg
    tile_expert = jnp.sum((tile_start[:, None] >= ends[None, :]).astype(I32), axis=1)
    last_expert = jnp.max(jnp.where(counts > 0, jnp.arange(n_experts, dtype=I32), 0))
    tile_expert = jnp.minimum(tile_expert, last_expert)
    return pos.astype(I32), row_tok, row_w, tile_expert, n_tiles


def _moe_ffn(x2, h3, top_i, top_w, w1, w3, w2, mod_l, gt_blk, S):
    T, D = x2.shape
    E = w1.shape[0]
    tmg = _tile(S, 512, SUBLANES)
    n_rows = TOP_K * T + E * tmg
    pos, row_tok, row_w, tile_expert, n_tiles = _route_tables(top_i, top_w, E, tmg, n_rows)
    act = _moe_swiglu(h3, row_tok, w1, w3, tile_expert, n_tiles, tmg)
    ys3 = _moe_down(act, w2, row_w, tile_expert, n_tiles, tmg)
    return _moe_combine(ys3, pos, x2, mod_l, gt_blk, S)


def kernel(x, c, w_ada, b_ada, norm_mix, norm_ffn, w_in, w_pool, pool_scale, lb_logits, hg_norm, w_br_pool, w_br_sb,
           w_br_hg, w_out, ffn_w1, ffn_w3, ffn_w2, w_router, b_router, moe_w1, moe_w3, moe_w2, final_norm):
    B, S, D = x.shape
    depth = w_in.shape[0]
    pw = pool_scale.shape[1]
    sbw = w_br_sb.shape[1]
    hgw = w_br_hg.shape[1]
    c_q_sb = pw
    c_k_sb = c_q_sb + sbw
    c_v_sb = c_k_sb + sbw
    c_q_hg = c_v_sb + sbw
    c_f_hg = c_q_hg + hgw
    c_i_hg = c_f_hg + hgw
    c_g_hg = c_i_hg + hgw
    c_gate = c_g_hg + hgw

    x2 = x.reshape(B * S, D)
    mod = _ada_mod(c, w_ada, b_ada)
    ffn_w2_bf16 = ffn_w2.astype(BF16)
    w_br_pool, w_br_sb, w_br_hg, w_out = (w.astype(BF16) for w in (w_br_pool, w_br_sb, w_br_hg, w_out))
    for l in range(depth):
        mod_l = mod[l].reshape(B, 1, N_MOD * D)
        h = _norm_mod(x2, norm_mix[l], mod_l, 1, 0, S)
        proj = _proj_matmul(h, w_in, l, ACT_DTYPE)
        y_pool = _pool_mixer(proj, w_pool[l].astype(BF16), pool_scale[l], B, S)
        y_sb = _sb_attention(proj, c_q_sb, c_k_sb, c_v_sb, sbw // HEAD, B, S)
        y_hg = _hgrn2(proj, c_q_hg, c_f_hg, c_i_hg, c_g_hg, lb_logits, hg_norm[l], l, hgw // HEAD, B, S)
        merged = _merge_matmul(y_pool, y_sb, y_hg, w_br_pool, w_br_sb, w_br_hg, l, proj, c_gate)
        x2 = _resid_matmul(merged, w_out, l, x2, mod_l, 2, S)
        j = l // 2
        if l % 2 == 0:
            h = _norm_mod(x2, norm_ffn[l], mod_l, 4, 3, S)
            act = _swiglu_matmul(h, ffn_w1, ffn_w3, j)
            x2 = _resid_matmul(act, ffn_w2_bf16, j, x2, mod_l, 5, S, tk_pref=5632)
        else:
            h3, idx, wgt = _norm_mod(x2, norm_ffn[l], mod_l, 4, 3, S, router=(w_router[j], b_router[j]))
            x2 = _moe_ffn(x2, h3, idx[:, :TOP_K], wgt[:, :TOP_K], moe_w1[j].astype(BF16), moe_w3[j].astype(BF16),
                          moe_w2[j].astype(BF16), mod_l, 5, S)
    return _final_norm(x2, final_norm).reshape(B, S, D)
```

```python
import functools
import math

import jax
import jax.numpy as jnp
from jax import lax
from jax.experimental import pallas as pl
from jax.experimental.pallas import tpu as pltpu

F32 = jnp.float32
BF16 = jnp.bfloat16
I32 = jnp.int32

EPS = 1e-6
HEAD = 128
POOL_WINDOWS = (2, 4, 8, 16)
POOL_HALO = 16
HG_CHUNK = 64
HG_SUB = 16
N_MOD = 6
TOP_K = 2
LANES = 128
SUBLANES = 8
VMEM_CAP = 60000 * 1024
ACT_DTYPE = BF16


def _tile(n, pref, mult=LANES):
    if n <= pref:
        return n
    t = (pref // mult) * mult
    while t >= mult:
        if n % t == 0:
            return t
        t -= mult
    return n


def _nbytes(shape, dtype):
    n = 1
    for s in shape:
        n *= s
    return n * jnp.dtype(dtype).itemsize


def _params(sems, blocks, scratch=()):
    need = 2 * sum(_nbytes(s, d) for s, d in blocks) + sum(_nbytes(s, d) for s, d in scratch)
    limit = min(VMEM_CAP, max(32 * 1024 * 1024, need + need // 4 + (4 << 20)))
    return pltpu.CompilerParams(dimension_semantics=sems, vmem_limit_bytes=int(limit))


def _split3(x):
    hi = x.astype(BF16)
    r1 = x - hi.astype(F32)
    mid = r1.astype(BF16)
    lo = (r1 - mid.astype(F32)).astype(BF16)
    return hi, mid, lo


def _dot(a, b):
    return jnp.dot(a, b, preferred_element_type=F32)


def _ada_kernel(c_ref, w_ref, b_ref, o_ref):
    ca = jax.nn.silu(c_ref[...])
    o_ref[0] = _dot(ca.astype(BF16), w_ref[0].astype(BF16)) + b_ref[0]


def _ada_mod(c, w_ada, b_ada):
    L, D, N = w_ada.shape
    B = c.shape[0]
    tn = _tile(N, 1024)
    blocks = [((B, D), F32), ((1, D, tn), w_ada.dtype), ((1, 1, tn), F32), ((1, B, tn), F32), ((D, tn), BF16)]
    return pl.pallas_call(
        _ada_kernel,
        grid=(L, N // tn),
        in_specs=[pl.BlockSpec((B, D), lambda l, j: (0, 0)),
                  pl.BlockSpec((1, D, tn), lambda l, j: (l, 0, j)),
                  pl.BlockSpec((1, 1, tn), lambda l, j: (l, 0, j))],
        out_specs=pl.BlockSpec((1, B, tn), lambda l, j: (l, 0, j)),
        out_shape=jax.ShapeDtypeStruct((L, B, N), F32),
        compiler_params=_params(("parallel", "parallel"), blocks),
        name="ada_mod",
    )(c, w_ada, b_ada.reshape(L, 1, N))


def _norm_mod_value(x_ref, g_ref, sc_ref, sh_ref):
    x = x_ref[...]
    ms = jnp.mean(x * x, axis=-1, keepdims=True)
    y = x * lax.rsqrt(ms + EPS) * g_ref[...]
    return y * (1.0 + sc_ref[0]) + sh_ref[0]


def _norm_mod_kernel(x_ref, g_ref, sc_ref, sh_ref, o_ref):
    o_ref[...] = _norm_mod_value(x_ref, g_ref, sc_ref, sh_ref).astype(o_ref.dtype)


def _norm_mod_router_kernel(x_ref, g_ref, sc_ref, sh_ref, wr_ref, br_ref, o_ref, idx_ref, wgt_ref, slab_ref, *,
                            n_experts):
    h = _norm_mod_value(x_ref, g_ref, sc_ref, sh_ref)
    _rows_to_slabs(h, slab_ref, o_ref)
    h_hi = h.astype(BF16)
    h_lo = (h - h_hi.astype(F32)).astype(BF16)
    logits = (_dot(h_hi, wr_ref[0]) + _dot(h_hi, wr_ref[1]) + _dot(h_lo, wr_ref[0])
              + _dot(h_hi, wr_ref[2]) + _dot(h_lo, wr_ref[1])) + br_ref[...]
    lane = lax.broadcasted_iota(I32, logits.shape, 1)
    lanef = lane.astype(F32)
    lg = jnp.where(lane < n_experts, logits, -jnp.inf)
    m1 = jnp.max(lg, axis=-1, keepdims=True)
    i1 = jnp.min(jnp.where(lg == m1, lanef, float(LANES)), axis=-1, keepdims=True)
    lg2 = jnp.where(lanef == i1, -jnp.inf, lg)
    m2 = jnp.max(lg2, axis=-1, keepdims=True)
    i2 = jnp.min(jnp.where(lg2 == m2, lanef, float(LANES)), axis=-1, keepdims=True)
    e2 = jnp.exp(m2 - m1)
    w1 = 1.0 / (1.0 + e2)
    w2 = e2 / (1.0 + e2)
    idx_ref[...] = jnp.where(lane == 0, i1, jnp.where(lane == 1, i2, 0.0)).astype(I32)
    wgt_ref[...] = jnp.where(lane == 0, w1, jnp.where(lane == 1, w2, 0.0))


def _norm_mod(x2, gain, mod_l, sc_blk, sh_blk, S, router=None):
    T, D = x2.shape
    tm = _tile(S, 256, SUBLANES)
    nps = S // tm
    in_specs = [pl.BlockSpec((tm, D), lambda i: (i, 0)),
                pl.BlockSpec((1, D), lambda i: (0, 0)),
                pl.BlockSpec((1, 1, D), lambda i: (i // nps, 0, sc_blk)),
                pl.BlockSpec((1, 1, D), lambda i: (i // nps, 0, sh_blk))]
    blocks = [((tm, D), F32), ((1, D), F32), ((1, 1, D), F32), ((1, 1, D), F32), ((tm, D), ACT_DTYPE)]
    h_spec = pl.BlockSpec((tm, D), lambda i: (i, 0))
    h_shape = jax.ShapeDtypeStruct((T, D), ACT_DTYPE)
    if router is None:
        return pl.pallas_call(
            _norm_mod_kernel, grid=(T // tm,), in_specs=in_specs, out_specs=h_spec, out_shape=h_shape,
            compiler_params=_params(("parallel",), blocks), name="norm_mod",
        )(x2, gain.reshape(1, D), mod_l, mod_l)
    w_router, b_router = router
    E = w_router.shape[1]
    db = _slab_pitch(D // LANES)
    wr = jnp.zeros((D, LANES), F32).at[:, :E].set(w_router)
    wr3 = jnp.stack(_split3(wr))
    br = jnp.zeros((1, LANES), F32).at[0, :E].set(b_router)
    in_specs += [pl.BlockSpec((3, D, LANES), lambda i: (0, 0, 0)), pl.BlockSpec((1, LANES), lambda i: (0, 0))]
    blocks += [((3, D, LANES), BF16), ((tm, LANES), F32), ((tm, LANES), F32), ((tm, D), F32)]
    scratch = [((tm * db, LANES), F32)]
    lane_spec = pl.BlockSpec((tm, LANES), lambda i: (i, 0))
    return pl.pallas_call(
        functools.partial(_norm_mod_router_kernel, n_experts=E),
        grid=(T // tm,), in_specs=in_specs,
        out_specs=[pl.BlockSpec((tm, db, LANES), lambda i: (i, 0, 0)), lane_spec, lane_spec],
        out_shape=[jax.ShapeDtypeStruct((T, db, LANES), F32), jax.ShapeDtypeStruct((T, LANES), I32),
                   jax.ShapeDtypeStruct((T, LANES), F32)],
        scratch_shapes=[pltpu.VMEM((tm * db, LANES), F32)],
        compiler_params=_params(("parallel",), blocks, scratch), name="norm_mod_router",
    )(x2, gain.reshape(1, D), mod_l, mod_l, wr3, br)


def _final_norm_kernel(x_ref, g_ref, o_ref):
    x = x_ref[...]
    ms = jnp.mean(x * x, axis=-1, keepdims=True)
    o_ref[...] = x * lax.rsqrt(ms + EPS) * g_ref[...]


def _final_norm(x2, gain):
    T, D = x2.shape
    tm = _tile(T, 256, SUBLANES)
    blocks = [((tm, D), F32), ((1, D), F32), ((tm, D), F32)]
    return pl.pallas_call(
        _final_norm_kernel, grid=(T // tm,),
        in_specs=[pl.BlockSpec((tm, D), lambda i: (i, 0)), pl.BlockSpec((1, D), lambda i: (0, 0))],
        out_specs=pl.BlockSpec((tm, D), lambda i: (i, 0)),
        out_shape=jax.ShapeDtypeStruct((T, D), F32),
        compiler_params=_params(("parallel",), blocks), name="final_norm",
    )(x2, gain.reshape(1, D))


def _proj_kernel(a_ref, w_ref, o_ref):
    o_ref[...] = _dot(a_ref[...], w_ref[0].astype(BF16)).astype(o_ref.dtype)


def _proj_matmul(a, w, l, out_dtype):
    M, K = a.shape
    N = w.shape[2]
    tm, tn = _tile(M, 2048), _tile(N, 512)
    blocks = [((K, tn), w.dtype), ((tm, tn), out_dtype)]
    scratch = [((tm, K), a.dtype), ((K, tn), BF16)]
    return pl.pallas_call(
        _proj_kernel, grid=(M // tm, N // tn),
        in_specs=[pl.BlockSpec((tm, K), lambda i, j: (i, 0), pipeline_mode=pl.Buffered(1)),
                  pl.BlockSpec((1, K, tn), lambda i, j: (l, 0, j))],
        out_specs=pl.BlockSpec((tm, tn), lambda i, j: (i, j)),
        out_shape=jax.ShapeDtypeStruct((M, N), out_dtype),
        compiler_params=_params(("parallel", "arbitrary"), blocks, scratch), name="proj_matmul",
    )(a, w)


def _swiglu_kernel(a_ref, w1_ref, w3_ref, o_ref):
    a = a_ref[...]
    up = _dot(a, w3_ref[0].astype(BF16))
    o_ref[...] = (jax.nn.silu(_dot(a, w1_ref[0].astype(BF16))) * up).astype(o_ref.dtype)


def _swiglu_matmul(a, w1, w3, l):
    M, K = a.shape
    N = w1.shape[2]
    tm, tn = _tile(M, 2048), _tile(N, 256)
    blocks = [((K, tn), w1.dtype), ((K, tn), w3.dtype), ((tm, tn), ACT_DTYPE)]
    scratch = [((tm, K), a.dtype), ((K, tn), BF16), ((K, tn), BF16)]
    return pl.pallas_call(
        _swiglu_kernel, grid=(M // tm, N // tn),
        in_specs=[pl.BlockSpec((tm, K), lambda i, j: (i, 0), pipeline_mode=pl.Buffered(1)),
                  pl.BlockSpec((1, K, tn), lambda i, j: (l, 0, j)),
                  pl.BlockSpec((1, K, tn), lambda i, j: (l, 0, j))],
        out_specs=pl.BlockSpec((tm, tn), lambda i, j: (i, j)),
        out_shape=jax.ShapeDtypeStruct((M, N), ACT_DTYPE),
        compiler_params=_params(("parallel", "arbitrary"), blocks, scratch), name="swiglu_matmul",
    )(a, w1, w3)


def _resid_kernel(a_ref, w_ref, x_ref, gt_ref, o_ref, acc_ref, *, nk):
    part = _dot(a_ref[...], w_ref[0].astype(BF16))
    if nk == 1:
        o_ref[...] = x_ref[...] + gt_ref[0] * part
        return
    k = pl.program_id(2)

    @pl.when(k == 0)
    def _():
        acc_ref[...] = part

    @pl.when(jnp.logical_and(k > 0, k < nk - 1))
    def _():
        acc_ref[...] += part

    @pl.when(k == nk - 1)
    def _():
        o_ref[...] = x_ref[...] + gt_ref[0] * (acc_ref[...] + part)


def _resid_matmul(a, w, l, x2, mod_l, gt_blk, S, tk_pref=4096):
    M, K = a.shape
    N = w.shape[2]
    tk = _tile(K, tk_pref)
    nk = K // tk
    tm, tn = _tile(S, 1024), _tile(N, 512)
    nps = S // tm
    gpb = N // tn
    blocks = [((tm, tk), a.dtype), ((tk, tn), w.dtype), ((tm, tn), F32), ((1, 1, tn), F32), ((tm, tn), F32)]
    scratch = [((tm, tn), F32), ((tk, tn), BF16)]
    return pl.pallas_call(
        functools.partial(_resid_kernel, nk=nk), grid=(M // tm, N // tn, nk),
        in_specs=[pl.BlockSpec((tm, tk), lambda i, j, k: (i, k)),
                  pl.BlockSpec((1, tk, tn), lambda i, j, k: (l, k, j)),
                  pl.BlockSpec((tm, tn), lambda i, j, k: (i, j)),
                  pl.BlockSpec((1, 1, tn), lambda i, j, k: (i // nps, 0, gt_blk * gpb + j))],
        out_specs=pl.BlockSpec((tm, tn), lambda i, j, k: (i, j)),
        out_shape=jax.ShapeDtypeStruct((M, N), F32),
        scratch_shapes=[pltpu.VMEM((tm, tn), F32)],
        compiler_params=_params(("parallel", "arbitrary", "arbitrary"), blocks, scratch), name="resid_matmul",
    )(a, w, x2, mod_l)


def _merge_kernel(yp_ref, ys_ref, yh_ref, wp_ref, ws_ref, wh_ref, g0_ref, g1_ref, g2_ref, o_ref):
    def gate(g_ref):
        return jax.nn.sigmoid(g_ref[...].astype(F32))
    m = (gate(g0_ref) * _dot(yp_ref[...], wp_ref[0].astype(BF16))
         + gate(g1_ref) * _dot(ys_ref[...], ws_ref[0].astype(BF16))
         + gate(g2_ref) * _dot(yh_ref[...], wh_ref[0].astype(BF16)))
    o_ref[...] = m.astype(o_ref.dtype)


def _merge_matmul(y_pool, y_sb, y_hg, w_p, w_s, w_h, l, proj, gate_col0):
    M = y_pool.shape[0]
    D = w_p.shape[2]
    tm, tn = _tile(M, 1024), _tile(math.gcd(D, gate_col0), 512)
    g0 = gate_col0 // tn
    gpb = D // tn
    ks = (y_pool.shape[1], y_sb.shape[1], y_hg.shape[1])
    blocks = ([((tm, k), ACT_DTYPE) for k in ks] + [((k, tn), w.dtype) for k, w in zip(ks, (w_p, w_s, w_h))]
              + [((tm, tn), proj.dtype)] * 3 + [((tm, tn), ACT_DTYPE)])
    a_specs = [pl.BlockSpec((tm, k), lambda i, j: (i, 0)) for k in ks]
    w_specs = [pl.BlockSpec((1, k, tn), lambda i, j: (l, 0, j)) for k in ks]
    g_specs = [pl.BlockSpec((tm, tn), functools.partial(lambda i, j, b: (i, g0 + b * gpb + j), b=b)) for b in range(3)]
    return pl.pallas_call(
        _merge_kernel, grid=(M // tm, D // tn),
        in_specs=a_specs + w_specs + g_specs,
        out_specs=pl.BlockSpec((tm, tn), lambda i, j: (i, j)),
        out_shape=jax.ShapeDtypeStruct((M, D), ACT_DTYPE),
        compiler_params=_params(("parallel", "arbitrary"), blocks), name="merge_matmul",
    )(y_pool, y_sb, y_hg, w_p, w_s, w_h, proj, proj, proj)


def _pool_kernel(u_ref, wp_ref, ps_ref, o_ref, ext_ref, *, ts, pc):
    si = pl.program_id(1)

    @pl.when(si == 0)
    def _():
        ext_ref[0:POOL_HALO, :] = jnp.zeros((POOL_HALO, ext_ref.shape[1]), F32)

    u = u_ref[...].astype(F32)
    ext_ref[POOL_HALO:POOL_HALO + ts, :] = u
    pos = si * ts + lax.broadcasted_iota(I32, (ts, 1), 0)
    for g, w in enumerate(POOL_WINDOWS):
        c0, c1 = g * pc, (g + 1) * pc
        ug = u[:, c0:c1]
        acc = ug
        for j in range(1, w):
            acc = acc + ext_ref[POOL_HALO - j:POOL_HALO - j + ts, c0:c1]
        cnt = jnp.minimum(pos + 1, w).astype(F32)
        pooled = acc / cnt - ug
        y = _dot(pooled.astype(BF16), wp_ref[g]) * ps_ref[:, c0:c1]
        o_ref[:, c0:c1] = y.astype(o_ref.dtype)
    ext_ref[0:POOL_HALO, :] = ext_ref[ts:ts + POOL_HALO, :]


def _pool_mixer(proj, w_pool, pool_scale, B, S):
    G, pc, _ = w_pool.shape
    pw = G * pc
    ts = _tile(S, 512, POOL_HALO)
    ns = S // ts
    blocks = [((ts, pw), proj.dtype), ((G, pc, pc), BF16), ((1, pw), F32), ((ts, pw), ACT_DTYPE)]
    scratch = [((ts + POOL_HALO, pw), F32)]
    return pl.pallas_call(
        functools.partial(_pool_kernel, ts=ts, pc=pc), grid=(B, ns),
        in_specs=[pl.BlockSpec((ts, pw), lambda b, s: (b * ns + s, 0)),
                  pl.BlockSpec((G, pc, pc), lambda b, s: (0, 0, 0)),
                  pl.BlockSpec((1, pw), lambda b, s: (0, 0))],
        out_specs=pl.BlockSpec((ts, pw), lambda b, s: (b * ns + s, 0)),
        out_shape=jax.ShapeDtypeStruct((B * S, pw), ACT_DTYPE),
        scratch_shapes=[pltpu.VMEM((ts + POOL_HALO, pw), F32)],
        compiler_params=_params(("parallel", "arbitrary"), blocks, scratch), name="pool_mixer",
    )(proj, w_pool, pool_scale.reshape(1, pw))


def _sb_kernel(q_ref, k_ref, v_ref, o_ref, *, tq, hb):
    qi = pl.program_id(2)
    scale = HEAD ** -0.5
    rows = lax.broadcasted_iota(I32, (tq, tq), 0)
    cols = lax.broadcasted_iota(I32, (tq, tq), 1)
    later = (rows > cols).astype(BF16)
    below = cols < rows
    qs = [q_ref[:, h * HEAD:(h + 1) * HEAD].astype(BF16) for h in range(hb)]

    def tiles(kj, carry, diagonal):
        k0 = pl.multiple_of(kj * tq, tq)
        heads = range(hb)
        zs = [lax.dot_general(qs[h], k_ref[pl.ds(k0, tq), h * HEAD:(h + 1) * HEAD].astype(BF16),
                              (((1,), (1,)), ((), ())), preferred_element_type=F32) * scale for h in heads]
        log_betas, log_keeps, splits = [], [], []
        for h in heads:
            z = zs[h]
            log_beta = jnp.minimum(z, 0.0) - jnp.log(1.0 + jnp.exp(-jnp.abs(z)))
            log_keep = log_beta - z
            if diagonal:
                log_keep = jnp.where(below, log_keep, 0.0)
            hi = log_keep.astype(BF16)
            lo = (log_keep - hi.astype(F32)).astype(BF16)
            log_betas.append(log_beta)
            log_keeps.append(log_keep)
            splits.append((hi, lo))
        afters = [_dot(splits[h][0], later) + _dot(splits[h][1], later) + carry[2 * h + 1] for h in heads]
        weights = []
        for h in heads:
            a = jnp.exp(log_betas[h] + afters[h])
            if diagonal:
                a = jnp.where(below, a, 0.0)
            weights.append(a.astype(BF16))
        out = []
        for h in heads:
            vs = v_ref[pl.ds(k0, tq), h * HEAD:(h + 1) * HEAD].astype(BF16)
            out.append(carry[2 * h] + _dot(weights[h], vs))
            out.append(carry[2 * h + 1] + jnp.sum(log_keeps[h], axis=1, keepdims=True))
        return tuple(out)

    zero = (jnp.zeros((tq, HEAD), F32), jnp.zeros((tq, 1), F32))
    carry = tiles(qi, zero * hb, True)
    carry = lax.fori_loop(1, qi + 1, lambda jj, c: tiles(qi - jj, c, False), carry)
    for h in range(hb):
        o_ref[:, h * HEAD:(h + 1) * HEAD] = carry[2 * h].astype(o_ref.dtype)


def _heads_per_step(n_heads, col0s, prefs):
    for hb in prefs:
        if n_heads % hb == 0 and all(c % (hb * HEAD) == 0 for c in col0s):
            return hb
    return 1


def _sb_attention(proj, q_col0, k_col0, v_col0, n_heads, B, S):
    tq = _tile(S, 256)
    nq = S // tq
    hb = _heads_per_step(n_heads, (q_col0, k_col0, v_col0), (4, 2))
    w = hb * HEAD
    qb, kb, vb = q_col0 // w, k_col0 // w, v_col0 // w
    blocks = [((tq, w), proj.dtype), ((S, w), proj.dtype), ((S, w), proj.dtype), ((tq, w), ACT_DTYPE)]
    return pl.pallas_call(
        functools.partial(_sb_kernel, tq=tq, hb=hb), grid=(B, n_heads // hb, nq),
        in_specs=[pl.BlockSpec((tq, w), lambda b, h, i: (b * nq + i, qb + h)),
                  pl.BlockSpec((S, w), lambda b, h, i: (b, kb + h)),
                  pl.BlockSpec((S, w), lambda b, h, i: (b, vb + h))],
        out_specs=pl.BlockSpec((tq, w), lambda b, h, i: (b * nq + i, h)),
        out_shape=jax.ShapeDtypeStruct((B * S, n_heads * HEAD), ACT_DTYPE),
        compiler_params=_params(("parallel", "parallel", "arbitrary"), blocks), name="sb_attention",
    )(proj, proj, proj)


def _hg_kernel(q_ref, f_ref, i_ref, g_ref, lbl_ref, gn_ref, o_ref, st_ref, b_ref, kk_ref, *, layer, n_chunks, hb):
    C = HG_CHUNK
    si = pl.program_id(2)

    @pl.when(si == 0)
    def _():
        st_ref[...] = jnp.zeros(st_ref.shape, F32)

    lbl = lbl_ref[...]
    e = jnp.exp(lbl - jnp.max(lbl, axis=0, keepdims=True))
    p = e / jnp.sum(e, axis=0, keepdims=True)
    lb = jnp.zeros((1, hb * HEAD), F32)
    for j in range(1, layer + 1):
        lb = lb + p[j:j + 1, :]
    gn = gn_ref[...]
    rows = lax.broadcasted_iota(I32, (C, C), 0)
    cols = lax.broadcasted_iota(I32, (C, C), 1)
    incl = (rows >= cols).astype(BF16)
    lane = lax.broadcasted_iota(I32, (SUBLANES, C), 1)
    sub = lax.broadcasted_iota(I32, (SUBLANES, HEAD), 0)
    nb = C // HG_SUB
    pairs = [(bi, bj) for bi in range(nb) for bj in range(bi)]

    def gates(h, r0, c):
        hs = slice(h * HEAD, (h + 1) * HEAD)
        c["hs"] = hs
        qh = q_ref[pl.ds(r0, C), hs].astype(F32)
        fh = f_ref[pl.ds(r0, C), hs].astype(F32)
        c["ih"] = i_ref[pl.ds(r0, C), hs].astype(BF16)
        c["gh"] = g_ref[pl.ds(r0, C), hs].astype(F32)
        lbh = lb[:, hs]
        f = lbh + (1.0 - lbh) * jax.nn.sigmoid(fh)
        c["kk"] = 1.0 - f
        c["qa"] = jax.nn.silu(qh)
        c["logf"] = _split3(jnp.log(f))

    def decay(h, r0, c):
        hi, mid, lo = c["logf"]
        c["b"] = _dot(incl, hi) + _dot(incl, mid) + _dot(incl, lo)

    def factors(h, r0, c):
        b, kk, qa = c["b"], c["kk"], c["qa"]
        b_ref[h] = b
        kk_ref[h] = kk
        edge = [jnp.zeros((1, HEAD), F32)] + [b[m * HG_SUB - 1:m * HG_SUB, :] for m in range(1, nb + 1)]
        b_start = jnp.concatenate([jnp.broadcast_to(edge[m], (HG_SUB, HEAD)) for m in range(nb)], axis=0)
        b_end = jnp.concatenate([jnp.broadcast_to(edge[m + 1], (HG_SUB, HEAD)) for m in range(nb)], axis=0)
        qd_sub = qa * jnp.exp(b - b_start)
        kd_sub = kk * jnp.exp(b_end - b)
        lhs = jnp.concatenate([qd_sub[bi * HG_SUB:(bi + 1) * HG_SUB, :] * jnp.exp(edge[bi] - edge[bj + 1])
                               for bi, bj in pairs], axis=0)
        c["lhs"] = lhs.astype(BF16)
        c["rhs"] = kd_sub.astype(BF16)
        c["qd"] = (qd_sub * jnp.exp(b_start)).astype(BF16)
        c["kd"] = (kd_sub * jnp.exp(edge[nb] - b_end)).astype(BF16)
        c["carry"] = jnp.exp(edge[nb])

    def cross_scores(h, r0, c):
        c["cross"] = lax.dot_general(c["lhs"], c["rhs"], (((1,), (1,)), ((), ())),
                                     preferred_element_type=F32)

    def scores(h, r0, c):
        b, qa, cross = c["b"], c["qa"], c["cross"]
        pieces = []
        for g in range(C // SUBLANES):
            t0 = g * SUBLANES
            bi = t0 // HG_SUB
            bt = b[t0:t0 + SUBLANES, :]
            qt = qa[t0:t0 + SUBLANES, :]
            sc = jnp.zeros((SUBLANES, C), F32)
            for p, (pi, pj) in enumerate(pairs):
                if pi == bi:
                    r = p * HG_SUB + (t0 - bi * HG_SUB)
                    in_block = jnp.logical_and(lane >= pj * HG_SUB, lane < (pj + 1) * HG_SUB)
                    sc = jnp.where(in_block, cross[r:r + SUBLANES, :], sc)
            for s in range(bi * HG_SUB, t0 + SUBLANES):
                d = bt - b_ref[h, pl.ds(s, 1), :]
                if s >= t0:
                    d = jnp.where(sub + t0 >= s, d, -jnp.inf)
                wgt = qt * kk_ref[h, pl.ds(s, 1), :] * jnp.exp(d)
                col = jnp.sum(wgt, axis=1, keepdims=True)
                sc = jnp.where(lane == s, col, sc)
            pieces.append(sc)
        c["scores"] = jnp.concatenate(pieces, axis=0).astype(BF16)

    def outputs(h, r0, c):
        ih = c["ih"]
        st = st_ref[h]
        o = _dot(c["scores"], ih)
        c["o"] = o + lax.dot_general(c["qd"], st.astype(BF16), (((1,), (1,)), ((), ())), preferred_element_type=F32)
        st_ref[h] = st * c["carry"] + lax.dot_general(ih, c["kd"], (((0,), (0,)), ((), ())),
                                                      preferred_element_type=F32)

    def finish(h, r0, c):
        o = c["o"]
        y = o * lax.rsqrt(jnp.mean(o * o, axis=-1, keepdims=True) + EPS) * gn
        o_ref[pl.ds(r0, C), c["hs"]] = (y * jax.nn.silu(c["gh"])).astype(o_ref.dtype)

    def chunk(ci, carry):
        r0 = pl.multiple_of(ci * C, C)
        live = [dict() for _ in range(hb)]
        for stage in (gates, decay, factors, cross_scores, scores, outputs, finish):
            for h in range(hb):
                stage(h, r0, live[h])
        return carry

    lax.fori_loop(0, n_chunks, chunk, 0)


def _hgrn2(proj, q_col0, f_col0, i_col0, g_col0, lb_logits, hg_norm, layer, n_heads, B, S):
    ts = _tile(S, 512, HG_CHUNK)
    ns = S // ts
    L = lb_logits.shape[0]
    hb = _heads_per_step(n_heads, (q_col0, f_col0, i_col0, g_col0), (4, 2))
    w = hb * HEAD
    qb, fb, ib, gb = q_col0 // w, f_col0 // w, i_col0 // w, g_col0 // w

    def col_spec(cb):
        return pl.BlockSpec((ts, w), lambda b, h, s: (b * ns + s, cb + h))

    blocks = [((ts, w), proj.dtype)] * 4 + [((L, w), F32), ((1, HEAD), F32), ((ts, w), ACT_DTYPE)]
    scratch = [((hb, HEAD, HEAD), F32), ((hb, HG_CHUNK, HEAD), F32), ((hb, HG_CHUNK, HEAD), F32)]
    return pl.pallas_call(
        functools.partial(_hg_kernel, layer=layer, n_chunks=ts // HG_CHUNK, hb=hb), grid=(B, n_heads // hb, ns),
        in_specs=[col_spec(qb), col_spec(fb), col_spec(ib), col_spec(gb),
                  pl.BlockSpec((L, w), lambda b, h, s: (0, h)),
                  pl.BlockSpec((1, HEAD), lambda b, h, s: (0, 0))],
        out_specs=pl.BlockSpec((ts, w), lambda b, h, s: (b * ns + s, h)),
        out_shape=jax.ShapeDtypeStruct((B * S, n_heads * HEAD), ACT_DTYPE),
        scratch_shapes=[pltpu.VMEM(s, d) for s, d in scratch],
        compiler_params=_params(("parallel", "parallel", "arbitrary"), blocks, scratch), name="hgrn2",
    )(proj, proj, proj, proj, lb_logits, hg_norm.reshape(1, HEAD))


def _slab_pitch(db):
    return db if (db // SUBLANES) % 2 == 1 else db + SUBLANES


def _moe_swiglu_kernel(te_ref, nt_ref, tok_ref, h_hbm, w1_ref, w3_ref, o_ref, buf_ref, a_ref, sem, *, tmg, db, nj):
    i = pl.program_id(0)
    j = pl.program_id(1)
    nt = nt_ref[0]
    pitch = _slab_pitch(db)

    def row_copy(tile, r, slot):
        dst = buf_ref.at[slot, pl.ds(pl.multiple_of(r * pitch, SUBLANES), db)]
        return pltpu.make_async_copy(h_hbm.at[tok_ref[tile * tmg + r], pl.ds(0, db)], dst, sem.at[slot])

    def start_gather(tile, slot, lo, hi):
        lax.fori_loop(lo, hi, lambda r, c: (row_copy(tile, r, slot).start(), c)[1], 0)

    def wait_gather(tile, slot):
        lax.fori_loop(0, tmg, lambda r, c: (row_copy(tile, r, slot).wait(), c)[1], 0)

    slot = lax.rem(i, 2)

    @pl.when(j == 0)
    def _():
        @pl.when(i == 0)
        def _():
            start_gather(0, 0, 0, tmg)

        @pl.when(i < nt)
        def _():
            wait_gather(i, slot)
            for cb in range(db):
                a_ref[:, cb * LANES:(cb + 1) * LANES] = buf_ref[slot, pl.ds(cb, tmg, stride=pitch), :].astype(BF16)

    @pl.when(i + 1 < nt)
    def _():
        share = -(-tmg // nj)
        start_gather(i + 1, 1 - slot, jnp.minimum(j * share, tmg), jnp.minimum((j + 1) * share, tmg))

    @pl.when(i < nt)
    def _():
        a = a_ref[...]
        o_ref[...] = (jax.nn.silu(_dot(a, w1_ref[0])) * _dot(a, w3_ref[0])).astype(o_ref.dtype)

    @pl.when(i >= nt)
    def _():
        o_ref[...] = jnp.zeros(o_ref.shape, o_ref.dtype)


def _moe_swiglu(h3, row_tok, w1, w3, tile_expert, n_tiles, tmg):
    P = row_tok.shape[0]
    K = w1.shape[1]
    db = K // LANES
    N = w1.shape[2]
    tn = _tile(N, 512)
    blocks = [((1, K, tn), BF16), ((1, K, tn), BF16), ((tmg, tn), ACT_DTYPE)]
    buf_shape = (2, tmg * _slab_pitch(db), LANES)
    scratch = [(buf_shape, F32), ((tmg, K), BF16)]

    def w_map(i, j, te, nt, tok):
        return (te[i], 0, jnp.where(i < nt[0], j, N // tn - 1))

    return pl.pallas_call(
        functools.partial(_moe_swiglu_kernel, tmg=tmg, db=db, nj=N // tn),
        grid_spec=pltpu.PrefetchScalarGridSpec(
            num_scalar_prefetch=3, grid=(P // tmg, N // tn),
            in_specs=[pl.BlockSpec(memory_space=pl.ANY), pl.BlockSpec((1, K, tn), w_map), pl.BlockSpec((1, K, tn), w_map)],
            out_specs=pl.BlockSpec((tmg, tn), lambda i, j, te, nt, tok: (i, j)),
            scratch_shapes=[pltpu.VMEM(buf_shape, F32), pltpu.VMEM((tmg, K), BF16),
                            pltpu.SemaphoreType.DMA((2,))]),
        out_shape=jax.ShapeDtypeStruct((P, N), ACT_DTYPE),
        compiler_params=_params(("arbitrary", "arbitrary"), blocks, scratch), name="moe_swiglu",
    )(tile_expert, n_tiles, row_tok, h3, w1, w3)


def _rows_to_slabs(val, slab_ref, o_ref):
    m, n = o_ref.shape[0], o_ref.shape[1]
    nv = val.shape[1] // LANES
    for c in range(nv):
        slab_ref[pl.ds(c, m, stride=n), :] = val[:, c * LANES:(c + 1) * LANES]
    for c in range(nv, n):
        slab_ref[pl.ds(c, m, stride=n), :] = jnp.zeros((m, LANES), val.dtype)
    o_ref[...] = slab_ref[...].reshape(m, n, LANES)


def _moe_down_kernel(te_ref, nt_ref, a_ref, w_ref, rw_ref, o_ref, slab_ref):
    active = pl.program_id(0) < nt_ref[0]

    @pl.when(active)
    def _():
        _rows_to_slabs(_dot(a_ref[...], w_ref[0]) * rw_ref[...], slab_ref, o_ref)

    @pl.when(jnp.logical_not(active))
    def _():
        o_ref[...] = jnp.zeros(o_ref.shape, o_ref.dtype)


def _moe_down(a, w2, row_w, tile_expert, n_tiles, tmg):
    P, K = a.shape
    N = w2.shape[2]
    tn = _tile(N, 1024)
    nb = tn // LANES
    blocks = [((tmg, K), a.dtype), ((1, K, tn), BF16), ((tmg, LANES), F32), ((tmg, tn), F32)]
    scratch = [((tmg * nb, LANES), F32)]

    def a_map(i, j, te, nt):
        return (jnp.minimum(i, nt[0] - 1), 0)

    def w_map(i, j, te, nt):
        return (te[i], 0, jnp.where(i < nt[0], j, N // tn - 1))

    return pl.pallas_call(
        _moe_down_kernel,
        grid_spec=pltpu.PrefetchScalarGridSpec(
            num_scalar_prefetch=2, grid=(P // tmg, N // tn),
            in_specs=[pl.BlockSpec((tmg, K), a_map), pl.BlockSpec((1, K, tn), w_map),
                      pl.BlockSpec((tmg, 1), lambda i, j, te, nt: (i, 0))],
            out_specs=pl.BlockSpec((tmg, nb, LANES), lambda i, j, te, nt: (i, j, 0)),
            scratch_shapes=[pltpu.VMEM((tmg * nb, LANES), F32)]),
        out_shape=jax.ShapeDtypeStruct((P, N // LANES, LANES), F32),
        compiler_params=_params(("arbitrary", "arbitrary"), blocks, scratch), name="moe_down",
    )(tile_expert, n_tiles, a, w2, row_w.reshape(P, 1))


def _combine_kernel(pos_ref, ys_hbm, x_ref, gt_ref, o_ref, buf_ref, sem, *, tt, db):
    i = pl.program_id(0)
    pitch = _slab_pitch(db)

    def row_copy(tile, r, k, slot):
        dst = buf_ref.at[slot, pl.ds(pl.multiple_of((k * tt + r) * pitch, SUBLANES), db)]
        return pltpu.make_async_copy(ys_hbm.at[pos_ref[TOP_K * (tile * tt + r) + k]], dst, sem.at[slot])

    def for_rows(fn):
        def body(r, c):
            for k in range(TOP_K):
                fn(r, k)
            return c
        lax.fori_loop(0, tt, body, 0)

    slot = lax.rem(i, 2)

    @pl.when(i == 0)
    def _():
        for_rows(lambda r, k: row_copy(0, r, k, 0).start())

    for_rows(lambda r, k: row_copy(i, r, k, slot).wait())

    @pl.when(i + 1 < pl.num_programs(0))
    def _():
        for_rows(lambda r, k: row_copy(i + 1, r, k, 1 - slot).start())

    for cb in range(db):
        y = buf_ref[slot, pl.ds(cb, tt, stride=pitch), :]
        for k in range(1, TOP_K):
            y = y + buf_ref[slot, pl.ds(k * tt * pitch + cb, tt, stride=pitch), :]
        c0 = cb * LANES
        o_ref[:, c0:c0 + LANES] = x_ref[:, c0:c0 + LANES] + gt_ref[0, :, c0:c0 + LANES] * y


def _moe_combine(ys3, pos, x2, mod_l, gt_blk, S):
    T, D = x2.shape
    db = D // LANES
    tt = _tile(S, 256, SUBLANES)
    nps = S // tt
    blocks = [((tt, D), F32), ((1, 1, D), F32), ((tt, D), F32)]
    buf_shape = (2, TOP_K * tt * _slab_pitch(db), LANES)
    scratch = [(buf_shape, F32)]
    return pl.pallas_call(
        functools.partial(_combine_kernel, tt=tt, db=db),
        grid_spec=pltpu.PrefetchScalarGridSpec(
            num_scalar_prefetch=1, grid=(T // tt,),
            in_specs=[pl.BlockSpec(memory_space=pl.ANY),
                      pl.BlockSpec((tt, D), lambda i, pos: (i, 0)),
                      pl.BlockSpec((1, 1, D), lambda i, pos: (i // nps, 0, gt_blk))],
            out_specs=pl.BlockSpec((tt, D), lambda i, pos: (i, 0)),
            scratch_shapes=[pltpu.VMEM(buf_shape, F32), pltpu.SemaphoreType.DMA((2,))]),
        out_shape=jax.ShapeDtypeStruct((T, D), F32),
        compiler_params=_params(("arbitrary",), blocks, scratch), name="moe_combine",
    )(pos, ys3, x2, mod_l)


def _route_tables(top_i, top_w, n_experts, tmg, n_rows):
    T = top_i.shape[0]
    e_flat = top_i.reshape(-1)
    onehot = (e_flat[:, None] == jnp.arange(n_experts, dtype=I32)[None, :]).astype(I32)
    csum = jnp.cumsum(onehot, axis=0)
    rank = jnp.take_along_axis(csum, e_flat[:, None], axis=1)[:, 0] - 1
    counts = csum[-1]
    padded = ((counts + tmg - 1) // tmg) * tmg
    ends = jnp.cumsum(padded)
    pos = (ends - padded)[e_flat] + rank
    n_tiles = (ends[-1] // tmg).astype(I32).reshape(1)
    row_tok = jnp.zeros((n_rows,), I32).at[pos].set(jnp.arange(TOP_K * T, dtype=I32) // TOP_K)
    row_w = jnp.zeros((n_rows,), F32).at[pos].set(top_w.reshape(-1))
    tile_start = jnp.arange(n_rows // tmg, dtype=I32) * tmg
    tile_expert = jnp.sum((tile_start[:, None] >= ends[None, :]).astype(I32), axis=1)
    last_expert = jnp.max(jnp.where(counts > 0, jnp.arange(n_experts, dtype=I32), 0))
    tile_expert = jnp.minimum(tile_expert, last_expert)
    return pos.astype(I32), row_tok, row_w, tile_expert, n_tiles


def _moe_ffn(x2, h3, top_i, top_w, w1, w3, w2, mod_l, gt_blk, S):
    T, D = x2.shape
    E = w1.shape[0]
    tmg = _tile(S, 512, SUBLANES)
    n_rows = TOP_K * T + E * tmg
    pos, row_tok, row_w, tile_expert, n_tiles = _route_tables(top_i, top_w, E, tmg, n_rows)
    act = _moe_swiglu(h3, row_tok, w1, w3, tile_expert, n_tiles, tmg)
    ys3 = _moe_down(act, w2, row_w, tile_expert, n_tiles, tmg)
    return _moe_combine(ys3, pos, x2, mod_l, gt_blk, S)


def kernel(x, c, w_ada, b_ada, norm_mix, norm_ffn, w_in, w_pool, pool_scale, lb_logits, hg_norm, w_br_pool, w_br_sb,
           w_br_hg, w_out, ffn_w1, ffn_w3, ffn_w2, w_router, b_router, moe_w1, moe_w3, moe_w2, final_norm):
    B, S, D = x.shape
    depth = w_in.shape[0]
    pw = pool_scale.shape[1]
    sbw = w_br_sb.shape[1]
    hgw = w_br_hg.shape[1]
    c_q_sb = pw
    c_k_sb = c_q_sb + sbw
    c_v_sb = c_k_sb + sbw
    c_q_hg = c_v_sb + sbw
    c_f_hg = c_q_hg + hgw
    c_i_hg = c_f_hg + hgw
    c_g_hg = c_i_hg + hgw
    c_gate = c_g_hg + hgw

    x2 = x.reshape(B * S, D)
    mod = _ada_mod(c, w_ada, b_ada)
    ffn_w2_bf16 = ffn_w2.astype(BF16)
    w_br_pool, w_br_sb, w_br_hg, w_out = (w.astype(BF16) for w in (w_br_pool, w_br_sb, w_br_hg, w_out))
    for l in range(depth):
        mod_l = mod[l].reshape(B, 1, N_MOD * D)
        h = _norm_mod(x2, norm_mix[l], mod_l, 1, 0, S)
        proj = _proj_matmul(h, w_in, l, ACT_DTYPE)
        y_pool = _pool_mixer(proj, w_pool[l].astype(BF16), pool_scale[l], B, S)
        y_sb = _sb_attention(proj, c_q_sb, c_k_sb, c_v_sb, sbw // HEAD, B, S)
        y_hg = _hgrn2(proj, c_q_hg, c_f_hg, c_i_hg, c_g_hg, lb_logits, hg_norm[l], l, hgw // HEAD, B, S)
        merged = _merge_matmul(y_pool, y_sb, y_hg, w_br_pool, w_br_sb, w_br_hg, l, proj, c_gate)
        x2 = _resid_matmul(merged, w_out, l, x2, mod_l, 2, S)
        j = l // 2
        if l % 2 == 0:
            h = _norm_mod(x2, norm_ffn[l], mod_l, 4, 3, S)
            act = _swiglu_matmul(h, ffn_w1, ffn_w3, j)
            x2 = _resid_matmul(act, ffn_w2_bf16, j, x2, mod_l, 5, S, tk_pref=5632)
        else:
            h3, idx, wgt = _norm_mod(x2, norm_ffn[l], mod_l, 4, 3, S, router=(w_router[j], b_router[j]))
            x2 = _moe_ffn(x2, h3, idx[:, :TOP_K], wgt[:, :TOP_K], moe_w1[j].astype(BF16), moe_w3[j].astype(BF16),
                          moe_w2[j].astype(BF16), mod_l, 5, S)
    return _final_norm(x2, final_norm).reshape(B, S, D)
```

```python
import functools
import math

import jax
import jax.numpy as jnp
from jax import lax
from jax.experimental import pallas as pl
from jax.experimental.pallas import tpu as pltpu

F32 = jnp.float32
BF16 = jnp.bfloat16
I32 = jnp.int32

EPS = 1e-6
HEAD = 128
POOL_WINDOWS = (2, 4, 8, 16)
POOL_HALO = 16
HG_CHUNK = 64
HG_SUB = 16
N_MOD = 6
TOP_K = 2
LANES = 128
SUBLANES = 8
VMEM_CAP = 60000 * 1024
ACT_DTYPE = BF16


def _tile(n, pref, mult=LANES):
    if n <= pref:
        return n
    t = (pref // mult) * mult
    while t >= mult:
        if n % t == 0:
            return t
        t -= mult
    return n


def _nbytes(shape, dtype):
    n = 1
    for s in shape:
        n *= s
    return n * jnp.dtype(dtype).itemsize


def _params(sems, blocks, scratch=()):
    need = 2 * sum(_nbytes(s, d) for s, d in blocks) + sum(_nbytes(s, d) for s, d in scratch)
    limit = min(VMEM_CAP, max(32 * 1024 * 1024, need + need // 4 + (4 << 20)))
    return pltpu.CompilerParams(dimension_semantics=sems, vmem_limit_bytes=int(limit))


def _split3(x):
    hi = x.astype(BF16)
    r1 = x - hi.astype(F32)
    mid = r1.astype(BF16)
    lo = (r1 - mid.astype(F32)).astype(BF16)
    return hi, mid, lo


def _dot(a, b):
    return jnp.dot(a, b, preferred_element_type=F32)


def _ada_kernel(c_ref, w_ref, b_ref, o_ref):
    ca = jax.nn.silu(c_ref[...])
    o_ref[0] = _dot(ca.astype(BF16), w_ref[0].astype(BF16)) + b_ref[0]


def _ada_mod(c, w_ada, b_ada):
    L, D, N = w_ada.shape
    B = c.shape[0]
    tn = _tile(N, 1024)
    blocks = [((B, D), F32), ((1, D, tn), w_ada.dtype), ((1, 1, tn), F32), ((1, B, tn), F32), ((D, tn), BF16)]
    return pl.pallas_call(
        _ada_kernel,
        grid=(L, N // tn),
        in_specs=[pl.BlockSpec((B, D), lambda l, j: (0, 0)),
                  pl.BlockSpec((1, D, tn), lambda l, j: (l, 0, j)),
                  pl.BlockSpec((1, 1, tn), lambda l, j: (l, 0, j))],
        out_specs=pl.BlockSpec((1, B, tn), lambda l, j: (l, 0, j)),
        out_shape=jax.ShapeDtypeStruct((L, B, N), F32),
        compiler_params=_params(("parallel", "parallel"), blocks),
        name="ada_mod",
    )(c, w_ada, b_ada.reshape(L, 1, N))


def _norm_mod_value(x_ref, g_ref, sc_ref, sh_ref):
    x = x_ref[...]
    ms = jnp.mean(x * x, axis=-1, keepdims=True)
    y = x * lax.rsqrt(ms + EPS) * g_ref[...]
    return y * (1.0 + sc_ref[0]) + sh_ref[0]


def _norm_mod_kernel(x_ref, g_ref, sc_ref, sh_ref, o_ref):
    o_ref[...] = _norm_mod_value(x_ref, g_ref, sc_ref, sh_ref).astype(o_ref.dtype)


def _norm_mod_router_kernel(x_ref, g_ref, sc_ref, sh_ref, wr_ref, br_ref, o_ref, idx_ref, wgt_ref, slab_ref, *,
                            n_experts):
    h = _norm_mod_value(x_ref, g_ref, sc_ref, sh_ref)
    _rows_to_slabs(h, slab_ref, o_ref)
    h_hi = h.astype(BF16)
    h_lo = (h - h_hi.astype(F32)).astype(BF16)
    logits = (_dot(h_hi, wr_ref[0]) + _dot(h_hi, wr_ref[1]) + _dot(h_lo, wr_ref[0])
              + _dot(h_hi, wr_ref[2]) + _dot(h_lo, wr_ref[1])) + br_ref[...]
    lane = lax.broadcasted_iota(I32, logits.shape, 1)
    lanef = lane.astype(F32)
    lg = jnp.where(lane < n_experts, logits, -jnp.inf)
    m1 = jnp.max(lg, axis=-1, keepdims=True)
    i1 = jnp.min(jnp.where(lg == m1, lanef, float(LANES)), axis=-1, keepdims=True)
    lg2 = jnp.where(lanef == i1, -jnp.inf, lg)
    m2 = jnp.max(lg2, axis=-1, keepdims=True)
    i2 = jnp.min(jnp.where(lg2 == m2, lanef, float(LANES)), axis=-1, keepdims=True)
    e2 = jnp.exp(m2 - m1)
    w1 = 1.0 / (1.0 + e2)
    w2 = e2 / (1.0 + e2)
    idx_ref[...] = jnp.where(lane == 0, i1, jnp.where(lane == 1, i2, 0.0)).astype(I32)
    wgt_ref[...] = jnp.where(lane == 0, w1, jnp.where(lane == 1, w2, 0.0))


def _norm_mod(x2, gain, mod_l, sc_blk, sh_blk, S, router=None):
    T, D = x2.shape
    tm = _tile(S, 256, SUBLANES)
    nps = S // tm
    in_specs = [pl.BlockSpec((tm, D), lambda i: (i, 0)),
                pl.BlockSpec((1, D), lambda i: (0, 0)),
                pl.BlockSpec((1, 1, D), lambda i: (i // nps, 0, sc_blk)),
                pl.BlockSpec((1, 1, D), lambda i: (i // nps, 0, sh_blk))]
    blocks = [((tm, D), F32), ((1, D), F32), ((1, 1, D), F32), ((1, 1, D), F32), ((tm, D), ACT_DTYPE)]
    h_spec = pl.BlockSpec((tm, D), lambda i: (i, 0))
    h_shape = jax.ShapeDtypeStruct((T, D), ACT_DTYPE)
    if router is None:
        return pl.pallas_call(
            _norm_mod_kernel, grid=(T // tm,), in_specs=in_specs, out_specs=h_spec, out_shape=h_shape,
            compiler_params=_params(("parallel",), blocks), name="norm_mod",
        )(x2, gain.reshape(1, D), mod_l, mod_l)
    w_router, b_router = router
    E = w_router.shape[1]
    db = _slab_pitch(D // LANES)
    wr = jnp.zeros((D, LANES), F32).at[:, :E].set(w_router)
    wr3 = jnp.stack(_split3(wr))
    br = jnp.zeros((1, LANES), F32).at[0, :E].set(b_router)
    in_specs += [pl.BlockSpec((3, D, LANES), lambda i: (0, 0, 0)), pl.BlockSpec((1, LANES), lambda i: (0, 0))]
    blocks += [((3, D, LANES), BF16), ((tm, LANES), F32), ((tm, LANES), F32), ((tm, D), F32)]
    scratch = [((tm * db, LANES), F32)]
    lane_spec = pl.BlockSpec((tm, LANES), lambda i: (i, 0))
    return pl.pallas_call(
        functools.partial(_norm_mod_router_kernel, n_experts=E),
        grid=(T // tm,), in_specs=in_specs,
        out_specs=[pl.BlockSpec((tm, db, LANES), lambda i: (i, 0, 0)), lane_spec, lane_spec],
        out_shape=[jax.ShapeDtypeStruct((T, db, LANES), F32), jax.ShapeDtypeStruct((T, LANES), I32),
                   jax.ShapeDtypeStruct((T, LANES), F32)],
        scratch_shapes=[pltpu.VMEM((tm * db, LANES), F32)],
        compiler_params=_params(("parallel",), blocks, scratch), name="norm_mod_router",
    )(x2, gain.reshape(1, D), mod_l, mod_l, wr3, br)


def _final_norm_kernel(x_ref, g_ref, o_ref):
    x = x_ref[...]
    ms = jnp.mean(x * x, axis=-1, keepdims=True)
    o_ref[...] = x * lax.rsqrt(ms + EPS) * g_ref[...]


def _final_norm(x2, gain):
    T, D = x2.shape
    tm = _tile(T, 256, SUBLANES)
    blocks = [((tm, D), F32), ((1, D), F32), ((tm, D), F32)]
    return pl.pallas_call(
        _final_norm_kernel, grid=(T // tm,),
        in_specs=[pl.BlockSpec((tm, D), lambda i: (i, 0)), pl.BlockSpec((1, D), lambda i: (0, 0))],
        out_specs=pl.BlockSpec((tm, D), lambda i: (i, 0)),
        out_shape=jax.ShapeDtypeStruct((T, D), F32),
        compiler_params=_params(("parallel",), blocks), name="final_norm",
    )(x2, gain.reshape(1, D))


def _proj_kernel(a_ref, w_ref, o_ref):
    o_ref[...] = _dot(a_ref[...], w_ref[0].astype(BF16)).astype(o_ref.dtype)


def _proj_matmul(a, w, l, out_dtype):
    M, K = a.shape
    N = w.shape[2]
    tm, tn = _tile(M, 2048), _tile(N, 512)
    blocks = [((K, tn), w.dtype), ((tm, tn), out_dtype)]
    scratch = [((tm, K), a.dtype), ((K, tn), BF16)]
    return pl.pallas_call(
        _proj_kernel, grid=(M // tm, N // tn),
        in_specs=[pl.BlockSpec((tm, K), lambda i, j: (i, 0), pipeline_mode=pl.Buffered(1)),
                  pl.BlockSpec((1, K, tn), lambda i, j: (l, 0, j))],
        out_specs=pl.BlockSpec((tm, tn), lambda i, j: (i, j)),
        out_shape=jax.ShapeDtypeStruct((M, N), out_dtype),
        compiler_params=_params(("parallel", "arbitrary"), blocks, scratch), name="proj_matmul",
    )(a, w)


def _swiglu_kernel(a_ref, w1_ref, w3_ref, o_ref):
    a = a_ref[...]
    up = _dot(a, w3_ref[0].astype(BF16))
    o_ref[...] = (jax.nn.silu(_dot(a, w1_ref[0].astype(BF16))) * up).astype(o_ref.dtype)


def _swiglu_matmul(a, w1, w3, l):
    M, K = a.shape
    N = w1.shape[2]
    tm, tn = _tile(M, 2048), _tile(N, 256)
    blocks = [((K, tn), w1.dtype), ((K, tn), w3.dtype), ((tm, tn), ACT_DTYPE)]
    scratch = [((tm, K), a.dtype), ((K, tn), BF16), ((K, tn), BF16)]
    return pl.pallas_call(
        _swiglu_kernel, grid=(M // tm, N // tn),
        in_specs=[pl.BlockSpec((tm, K), lambda i, j: (i, 0), pipeline_mode=pl.Buffered(1)),
                  pl.BlockSpec((1, K, tn), lambda i, j: (l, 0, j)),
                  pl.BlockSpec((1, K, tn), lambda i, j: (l, 0, j))],
        out_specs=pl.BlockSpec((tm, tn), lambda i, j: (i, j)),
        out_shape=jax.ShapeDtypeStruct((M, N), ACT_DTYPE),
        compiler_params=_params(("parallel", "arbitrary"), blocks, scratch), name="swiglu_matmul",
    )(a, w1, w3)


def _resid_kernel(a_ref, w_ref, x_ref, gt_ref, o_ref, acc_ref, *, nk):
    part = _dot(a_ref[...], w_ref[0].astype(BF16))
    if nk == 1:
        o_ref[...] = x_ref[...] + gt_ref[0] * part
        return
    k = pl.program_id(2)

    @pl.when(k == 0)
    def _():
        acc_ref[...] = part

    @pl.when(jnp.logical_and(k > 0, k < nk - 1))
    def _():
        acc_ref[...] += part

    @pl.when(k == nk - 1)
    def _():
        o_ref[...] = x_ref[...] + gt_ref[0] * (acc_ref[...] + part)


def _resid_matmul(a, w, l, x2, mod_l, gt_blk, S, tk_pref=4096):
    M, K = a.shape
    N = w.shape[2]
    tk = _tile(K, tk_pref)
    nk = K // tk
    tm, tn = _tile(S, 1024), _tile(N, 512)
    nps = S // tm
    gpb = N // tn
    blocks = [((tm, tk), a.dtype), ((tk, tn), w.dtype), ((tm, tn), F32), ((1, 1, tn), F32), ((tm, tn), F32)]
    scratch = [((tm, tn), F32), ((tk, tn), BF16)]
    return pl.pallas_call(
        functools.partial(_resid_kernel, nk=nk), grid=(M // tm, N // tn, nk),
        in_specs=[pl.BlockSpec((tm, tk), lambda i, j, k: (i, k)),
                  pl.BlockSpec((1, tk, tn), lambda i, j, k: (l, k, j)),
                  pl.BlockSpec((tm, tn), lambda i, j, k: (i, j)),
                  pl.BlockSpec((1, 1, tn), lambda i, j, k: (i // nps, 0, gt_blk * gpb + j))],
        out_specs=pl.BlockSpec((tm, tn), lambda i, j, k: (i, j)),
        out_shape=jax.ShapeDtypeStruct((M, N), F32),
        scratch_shapes=[pltpu.VMEM((tm, tn), F32)],
        compiler_params=_params(("parallel", "arbitrary", "arbitrary"), blocks, scratch), name="resid_matmul",
    )(a, w, x2, mod_l)


def _merge_kernel(yp_ref, ys_ref, yh_ref, wp_ref, ws_ref, wh_ref, g0_ref, g1_ref, g2_ref, o_ref):
    def gate(g_ref):
        return jax.nn.sigmoid(g_ref[...].astype(F32))
    m = (gate(g0_ref) * _dot(yp_ref[...], wp_ref[0].astype(BF16))
         + gate(g1_ref) * _dot(ys_ref[...], ws_ref[0].astype(BF16))
         + gate(g2_ref) * _dot(yh_ref[...], wh_ref[0].astype(BF16)))
    o_ref[...] = m.astype(o_ref.dtype)


def _merge_matmul(y_pool, y_sb, y_hg, w_p, w_s, w_h, l, proj, gate_col0):
    M = y_pool.shape[0]
    D = w_p.shape[2]
    tm, tn = _tile(M, 1024), _tile(math.gcd(D, gate_col0), 512)
    g0 = gate_col0 // tn
    gpb = D // tn
    ks = (y_pool.shape[1], y_sb.shape[1], y_hg.shape[1])
    blocks = ([((tm, k), ACT_DTYPE) for k in ks] + [((k, tn), w.dtype) for k, w in zip(ks, (w_p, w_s, w_h))]
              + [((tm, tn), proj.dtype)] * 3 + [((tm, tn), ACT_DTYPE)])
    a_specs = [pl.BlockSpec((tm, k), lambda i, j: (i, 0)) for k in ks]
    w_specs = [pl.BlockSpec((1, k, tn), lambda i, j: (l, 0, j)) for k in ks]
    g_specs = [pl.BlockSpec((tm, tn), functools.partial(lambda i, j, b: (i, g0 + b * gpb + j), b=b)) for b in range(3)]
    return pl.pallas_call(
        _merge_kernel, grid=(M // tm, D // tn),
        in_specs=a_specs + w_specs + g_specs,
        out_specs=pl.BlockSpec((tm, tn), lambda i, j: (i, j)),
        out_shape=jax.ShapeDtypeStruct((M, D), ACT_DTYPE),
        compiler_params=_params(("parallel", "arbitrary"), blocks), name="merge_matmul",
    )(y_pool, y_sb, y_hg, w_p, w_s, w_h, proj, proj, proj)


def _pool_kernel(u_ref, wp_ref, ps_ref, o_ref, ext_ref, *, ts, pc):
    si = pl.program_id(1)

    @pl.when(si == 0)
    def _():
        ext_ref[0:POOL_HALO, :] = jnp.zeros((POOL_HALO, ext_ref.shape[1]), F32)

    u = u_ref[...].astype(F32)
    ext_ref[POOL_HALO:POOL_HALO + ts, :] = u
    pos = si * ts + lax.broadcasted_iota(I32, (ts, 1), 0)
    for g, w in enumerate(POOL_WINDOWS):
        c0, c1 = g * pc, (g + 1) * pc
        ug = u[:, c0:c1]
        acc = ug
        for j in range(1, w):
            acc = acc + ext_ref[POOL_HALO - j:POOL_HALO - j + ts, c0:c1]
        cnt = jnp.minimum(pos + 1, w).astype(F32)
        pooled = acc / cnt - ug
        y = _dot(pooled.astype(BF16), wp_ref[g]) * ps_ref[:, c0:c1]
        o_ref[:, c0:c1] = y.astype(o_ref.dtype)
    ext_ref[0:POOL_HALO, :] = ext_ref[ts:ts + POOL_HALO, :]


def _pool_mixer(proj, w_pool, pool_scale, B, S):
    G, pc, _ = w_pool.shape
    pw = G * pc
    ts = _tile(S, 512, POOL_HALO)
    ns = S // ts
    blocks = [((ts, pw), proj.dtype), ((G, pc, pc), BF16), ((1, pw), F32), ((ts, pw), ACT_DTYPE)]
    scratch = [((ts + POOL_HALO, pw), F32)]
    return pl.pallas_call(
        functools.partial(_pool_kernel, ts=ts, pc=pc), grid=(B, ns),
        in_specs=[pl.BlockSpec((ts, pw), lambda b, s: (b * ns + s, 0)),
                  pl.BlockSpec((G, pc, pc), lambda b, s: (0, 0, 0)),
                  pl.BlockSpec((1, pw), lambda b, s: (0, 0))],
        out_specs=pl.BlockSpec((ts, pw), lambda b, s: (b * ns + s, 0)),
        out_shape=jax.ShapeDtypeStruct((B * S, pw), ACT_DTYPE),
        scratch_shapes=[pltpu.VMEM((ts + POOL_HALO, pw), F32)],
        compiler_params=_params(("parallel", "arbitrary"), blocks, scratch), name="pool_mixer",
    )(proj, w_pool, pool_scale.reshape(1, pw))


def _sb_kernel(q_ref, k_ref, v_ref, o_ref, *, tq, hb):
    qi = pl.program_id(2)
    scale = HEAD ** -0.5
    rows = lax.broadcasted_iota(I32, (tq, tq), 0)
    cols = lax.broadcasted_iota(I32, (tq, tq), 1)
    later = (rows > cols).astype(BF16)
    below = cols < rows
    qs = [q_ref[:, h * HEAD:(h + 1) * HEAD].astype(BF16) for h in range(hb)]

    def tiles(kj, carry, diagonal):
        k0 = pl.multiple_of(kj * tq, tq)
        heads = range(hb)
        zs = [lax.dot_general(qs[h], k_ref[pl.ds(k0, tq), h * HEAD:(h + 1) * HEAD].astype(BF16),
                              (((1,), (1,)), ((), ())), preferred_element_type=F32) * scale for h in heads]
        log_betas, log_keeps, splits = [], [], []
        for h in heads:
            z = zs[h]
            log_beta = jnp.minimum(z, 0.0) - jnp.log(1.0 + jnp.exp(-jnp.abs(z)))
            log_keep = log_beta - z
            if diagonal:
                log_keep = jnp.where(below, log_keep, 0.0)
            hi = log_keep.astype(BF16)
            lo = (log_keep - hi.astype(F32)).astype(BF16)
            log_betas.append(log_beta)
            log_keeps.append(log_keep)
            splits.append((hi, lo))
        afters = [_dot(splits[h][0], later) + _dot(splits[h][1], later) + carry[2 * h + 1] for h in heads]
        weights = []
        for h in heads:
            a = jnp.exp(log_betas[h] + afters[h])
            if diagonal:
                a = jnp.where(below, a, 0.0)
            weights.append(a.astype(BF16))
        out = []
        for h in heads:
            vs = v_ref[pl.ds(k0, tq), h * HEAD:(h + 1) * HEAD].astype(BF16)
            out.append(carry[2 * h] + _dot(weights[h], vs))
            out.append(carry[2 * h + 1] + jnp.sum(log_keeps[h], axis=1, keepdims=True))
        return tuple(out)

    zero = (jnp.zeros((tq, HEAD), F32), jnp.zeros((tq, 1), F32))
    carry = tiles(qi, zero * hb, True)
    carry = lax.fori_loop(1, qi + 1, lambda jj, c: tiles(qi - jj, c, False), carry)
    for h in range(hb):
        o_ref[:, h * HEAD:(h + 1) * HEAD] = carry[2 * h].astype(o_ref.dtype)


def _heads_per_step(n_heads, col0s, prefs):
    for hb in prefs:
        if n_heads % hb == 0 and all(c % (hb * HEAD) == 0 for c in col0s):
            return hb
    return 1


def _sb_attention(proj, q_col0, k_col0, v_col0, n_heads, B, S):
    tq = _tile(S, 256)
    nq = S // tq
    hb = _heads_per_step(n_heads, (q_col0, k_col0, v_col0), (4, 2))
    w = hb * HEAD
    qb, kb, vb = q_col0 // w, k_col0 // w, v_col0 // w
    blocks = [((tq, w), proj.dtype), ((S, w), proj.dtype), ((S, w), proj.dtype), ((tq, w), ACT_DTYPE)]
    return pl.pallas_call(
        functools.partial(_sb_kernel, tq=tq, hb=hb), grid=(B, n_heads // hb, nq),
        in_specs=[pl.BlockSpec((tq, w), lambda b, h, i: (b * nq + i, qb + h)),
                  pl.BlockSpec((S, w), lambda b, h, i: (b, kb + h)),
                  pl.BlockSpec((S, w), lambda b, h, i: (b, vb + h))],
        out_specs=pl.BlockSpec((tq, w), lambda b, h, i: (b * nq + i, h)),
        out_shape=jax.ShapeDtypeStruct((B * S, n_heads * HEAD), ACT_DTYPE),
        compiler_params=_params(("parallel", "parallel", "arbitrary"), blocks), name="sb_attention",
    )(proj, proj, proj)


def _hg_kernel(q_ref, f_ref, i_ref, g_ref, lbl_ref, gn_ref, o_ref, st_ref, b_ref, kk_ref, *, layer, n_chunks, hb):
    C = HG_CHUNK
    si = pl.program_id(2)

    @pl.when(si == 0)
    def _():
        st_ref[...] = jnp.zeros(st_ref.shape, F32)

    lbl = lbl_ref[...]
    e = jnp.exp(lbl - jnp.max(lbl, axis=0, keepdims=True))
    p = e / jnp.sum(e, axis=0, keepdims=True)
    lb = jnp.zeros((1, hb * HEAD), F32)
    for j in range(1, layer + 1):
        lb = lb + p[j:j + 1, :]
    gn = gn_ref[...]
    rows = lax.broadcasted_iota(I32, (C, C), 0)
    cols = lax.broadcasted_iota(I32, (C, C), 1)
    incl = (rows >= cols).astype(BF16)
    lane = lax.broadcasted_iota(I32, (SUBLANES, C), 1)
    sub = lax.broadcasted_iota(I32, (SUBLANES, HEAD), 0)
    nb = C // HG_SUB
    pairs = [(bi, bj) for bi in range(nb) for bj in range(bi)]

    def gates(h, r0, c):
        hs = slice(h * HEAD, (h + 1) * HEAD)
        c["hs"] = hs
        qh = q_ref[pl.ds(r0, C), hs].astype(F32)
        fh = f_ref[pl.ds(r0, C), hs].astype(F32)
        c["ih"] = i_ref[pl.ds(r0, C), hs].astype(BF16)
        c["gh"] = g_ref[pl.ds(r0, C), hs].astype(F32)
        lbh = lb[:, hs]
        f = lbh + (1.0 - lbh) * jax.nn.sigmoid(fh)
        c["kk"] = 1.0 - f
        c["qa"] = jax.nn.silu(qh)
        c["logf"] = _split3(jnp.log(f))

    def decay(h, r0, c):
        hi, mid, lo = c["logf"]
        c["b"] = _dot(incl, hi) + _dot(incl, mid) + _dot(incl, lo)

    def factors(h, r0, c):
        b, kk, qa = c["b"], c["kk"], c["qa"]
        b_ref[h] = b
        kk_ref[h] = kk
        edge = [jnp.zeros((1, HEAD), F32)] + [b[m * HG_SUB - 1:m * HG_SUB, :] for m in range(1, nb + 1)]
        b_start = jnp.concatenate([jnp.broadcast_to(edge[m], (HG_SUB, HEAD)) for m in range(nb)], axis=0)
        b_end = jnp.concatenate([jnp.broadcast_to(edge[m + 1], (HG_SUB, HEAD)) for m in range(nb)], axis=0)
        qd_sub = qa * jnp.exp(b - b_start)
        kd_sub = kk * jnp.exp(b_end - b)
        lhs = jnp.concatenate([qd_sub[bi * HG_SUB:(bi + 1) * HG_SUB, :] * jnp.exp(edge[bi] - edge[bj + 1])
                               for bi, bj in pairs], axis=0)
        c["lhs"] = lhs.astype(BF16)
        c["rhs"] = kd_sub.astype(BF16)
        c["qd"] = (qd_sub * jnp.exp(b_start)).astype(BF16)
        c["kd"] = (kd_sub * jnp.exp(edge[nb] - b_end)).astype(BF16)
        c["carry"] = jnp.exp(edge[nb])

    def cross_scores(h, r0, c):
        c["cross"] = lax.dot_general(c["lhs"], c["rhs"], (((1,), (1,)), ((), ())),
                                     preferred_element_type=F32)

    def scores(h, r0, c):
        b, qa, cross = c["b"], c["qa"], c["cross"]
        pieces = []
        for g in range(C // SUBLANES):
            t0 = g * SUBLANES
            bi = t0 // HG_SUB
            bt = b[t0:t0 + SUBLANES, :]
            qt = qa[t0:t0 + SUBLANES, :]
            sc = jnp.zeros((SUBLANES, C), F32)
            for p, (pi, pj) in enumerate(pairs):
                if pi == bi:
                    r = p * HG_SUB + (t0 - bi * HG_SUB)
                    in_block = jnp.logical_and(lane >= pj * HG_SUB, lane < (pj + 1) * HG_SUB)
                    sc = jnp.where(in_block, cross[r:r + SUBLANES, :], sc)
            for s in range(bi * HG_SUB, t0 + SUBLANES):
                d = bt - b_ref[h, pl.ds(s, 1), :]
                if s >= t0:
                    d = jnp.where(sub + t0 >= s, d, -jnp.inf)
                wgt = qt * kk_ref[h, pl.ds(s, 1), :] * jnp.exp(d)
                col = jnp.sum(wgt, axis=1, keepdims=True)
                sc = jnp.where(lane == s, col, sc)
            pieces.append(sc)
        c["scores"] = jnp.concatenate(pieces, axis=0).astype(BF16)

    def outputs(h, r0, c):
        ih = c["ih"]
        st = st_ref[h]
        o = _dot(c["scores"], ih)
        c["o"] = o + lax.dot_general(c["qd"], st.astype(BF16), (((1,), (1,)), ((), ())), preferred_element_type=F32)
        st_ref[h] = st * c["carry"] + lax.dot_general(ih, c["kd"], (((0,), (0,)), ((), ())),
                                                      preferred_element_type=F32)

    def finish(h, r0, c):
        o = c["o"]
        y = o * lax.rsqrt(jnp.mean(o * o, axis=-1, keepdims=True) + EPS) * gn
        o_ref[pl.ds(r0, C), c["hs"]] = (y * jax.nn.silu(c["gh"])).astype(o_ref.dtype)

    def chunk(ci, carry):
        r0 = pl.multiple_of(ci * C, C)
        live = [dict() for _ in range(hb)]
        for stage in (gates, decay, factors, cross_scores, scores, outputs, finish):
            for h in range(hb):
                stage(h, r0, live[h])
        return carry

    lax.fori_loop(0, n_chunks, chunk, 0)


def _hgrn2(proj, q_col0, f_col0, i_col0, g_col0, lb_logits, hg_norm, layer, n_heads, B, S):
    ts = _tile(S, 512, HG_CHUNK)
    ns = S // ts
    L = lb_logits.shape[0]
    hb = _heads_per_step(n_heads, (q_col0, f_col0, i_col0, g_col0), (4, 2))
    w = hb * HEAD
    qb, fb, ib, gb = q_col0 // w, f_col0 // w, i_col0 // w, g_col0 // w

    def col_spec(cb):
        return pl.BlockSpec((ts, w), lambda b, h, s: (b * ns + s, cb + h))

    blocks = [((ts, w), proj.dtype)] * 4 + [((L, w), F32), ((1, HEAD), F32), ((ts, w), ACT_DTYPE)]
    scratch = [((hb, HEAD, HEAD), F32), ((hb, HG_CHUNK, HEAD), F32), ((hb, HG_CHUNK, HEAD), F32)]
    return pl.pallas_call(
        functools.partial(_hg_kernel, layer=layer, n_chunks=ts // HG_CHUNK, hb=hb), grid=(B, n_heads // hb, ns),
        in_specs=[col_spec(qb), col_spec(fb), col_spec(ib), col_spec(gb),
                  pl.BlockSpec((L, w), lambda b, h, s: (0, h)),
                  pl.BlockSpec((1, HEAD), lambda b, h, s: (0, 0))],
        out_specs=pl.BlockSpec((ts, w), lambda b, h, s: (b * ns + s, h)),
        out_shape=jax.ShapeDtypeStruct((B * S, n_heads * HEAD), ACT_DTYPE),
        scratch_shapes=[pltpu.VMEM(s, d) for s, d in scratch],
        compiler_params=_params(("parallel", "parallel", "arbitrary"), blocks, scratch), name="hgrn2",
    )(proj, proj, proj, proj, lb_logits, hg_norm.reshape(1, HEAD))


def _slab_pitch(db):
    return db if (db // SUBLANES) % 2 == 1 else db + SUBLANES


def _moe_swiglu_kernel(te_ref, nt_ref, tok_ref, h_hbm, w1_ref, w3_ref, o_ref, buf_ref, a_ref, sem, *, tmg, db, nj):
    i = pl.program_id(0)
    j = pl.program_id(1)
    nt = nt_ref[0]
    pitch = _slab_pitch(db)

    def row_copy(tile, r, slot):
        dst = buf_ref.at[slot, pl.ds(pl.multiple_of(r * pitch, SUBLANES), db)]
        return pltpu.make_async_copy(h_hbm.at[tok_ref[tile * tmg + r], pl.ds(0, db)], dst, sem.at[slot])

    def start_gather(tile, slot, lo, hi):
        lax.fori_loop(lo, hi, lambda r, c: (row_copy(tile, r, slot).start(), c)[1], 0)

    def wait_gather(tile, slot):
        lax.fori_loop(0, tmg, lambda r, c: (row_copy(tile, r, slot).wait(), c)[1], 0)

    slot = lax.rem(i, 2)

    @pl.when(j == 0)
    def _():
        @pl.when(i == 0)
        def _():
            start_gather(0, 0, 0, tmg)

        @pl.when(i < nt)
        def _():
            wait_gather(i, slot)
            for cb in range(db):
                a_ref[:, cb * LANES:(cb + 1) * LANES] = buf_ref[slot, pl.ds(cb, tmg, stride=pitch), :].astype(BF16)

    @pl.when(i + 1 < nt)
    def _():
        share = -(-tmg // nj)
        start_gather(i + 1, 1 - slot, jnp.minimum(j * share, tmg), jnp.minimum((j + 1) * share, tmg))

    @pl.when(i < nt)
    def _():
        a = a_ref[...]
        o_ref[...] = (jax.nn.silu(_dot(a, w1_ref[0])) * _dot(a, w3_ref[0])).astype(o_ref.dtype)

    @pl.when(i >= nt)
    def _():
        o_ref[...] = jnp.zeros(o_ref.shape, o_ref.dtype)


def _moe_swiglu(h3, row_tok, w1, w3, tile_expert, n_tiles, tmg):
    P = row_tok.shape[0]
    K = w1.shape[1]
    db = K // LANES
    N = w1.shape[2]
    tn = _tile(N, 512)
    blocks = [((1, K, tn), BF16), ((1, K, tn), BF16), ((tmg, tn), ACT_DTYPE)]
    buf_shape = (2, tmg * _slab_pitch(db), LANES)
    scratch = [(buf_shape, F32), ((tmg, K), BF16)]

    def w_map(i, j, te, nt, tok):
        return (te[i], 0, jnp.where(i < nt[0], j, N // tn - 1))

    return pl.pallas_call(
        functools.partial(_moe_swiglu_kernel, tmg=tmg, db=db, nj=N // tn),
        grid_spec=pltpu.PrefetchScalarGridSpec(
            num_scalar_prefetch=3, grid=(P // tmg, N // tn),
            in_specs=[pl.BlockSpec(memory_space=pl.ANY), pl.BlockSpec((1, K, tn), w_map), pl.BlockSpec((1, K, tn), w_map)],
            out_specs=pl.BlockSpec((tmg, tn), lambda i, j, te, nt, tok: (i, j)),
            scratch_shapes=[pltpu.VMEM(buf_shape, F32), pltpu.VMEM((tmg, K), BF16),
                            pltpu.SemaphoreType.DMA((2,))]),
        out_shape=jax.ShapeDtypeStruct((P, N), ACT_DTYPE),
        compiler_params=_params(("arbitrary", "arbitrary"), blocks, scratch), name="moe_swiglu",
    )(tile_expert, n_tiles, row_tok, h3, w1, w3)


def _rows_to_slabs(val, slab_ref, o_ref):
    m, n = o_ref.shape[0], o_ref.shape[1]
    nv = val.shape[1] // LANES
    for c in range(nv):
        slab_ref[pl.ds(c, m, stride=n), :] = val[:, c * LANES:(c + 1) * LANES]
    for c in range(nv, n):
        slab_ref[pl.ds(c, m, stride=n), :] = jnp.zeros((m, LANES), val.dtype)
    o_ref[...] = slab_ref[...].reshape(m, n, LANES)


def _moe_down_kernel(te_ref, nt_ref, a_ref, w_ref, o_ref, slab_ref):
    active = pl.program_id(0) < nt_ref[0]

    @pl.when(active)
    def _():
        _rows_to_slabs(_dot(a_ref[...], w_ref[0]), slab_ref, o_ref)

    @pl.when(jnp.logical_not(active))
    def _():
        o_ref[...] = jnp.zeros(o_ref.shape, o_ref.dtype)


def _moe_down(a, w2, tile_expert, n_tiles, tmg):
    P, K = a.shape
    N = w2.shape[2]
    tn = _tile(N, 1024)
    nb = tn // LANES
    blocks = [((tmg, K), a.dtype), ((1, K, tn), BF16), ((tmg, tn), F32)]
    scratch = [((tmg * nb, LANES), F32)]

    def a_map(i, j, te, nt):
        return (jnp.minimum(i, nt[0] - 1), 0)

    def w_map(i, j, te, nt):
        return (te[i], 0, jnp.where(i < nt[0], j, N // tn - 1))

    return pl.pallas_call(
        _moe_down_kernel,
        grid_spec=pltpu.PrefetchScalarGridSpec(
            num_scalar_prefetch=2, grid=(P // tmg, N // tn),
            in_specs=[pl.BlockSpec((tmg, K), a_map), pl.BlockSpec((1, K, tn), w_map)],
            out_specs=pl.BlockSpec((tmg, nb, LANES), lambda i, j, te, nt: (i, j, 0)),
            scratch_shapes=[pltpu.VMEM((tmg * nb, LANES), F32)]),
        out_shape=jax.ShapeDtypeStruct((P, N // LANES, LANES), F32),
        compiler_params=_params(("arbitrary", "arbitrary"), blocks, scratch), name="moe_down",
    )(tile_expert, n_tiles, a, w2)


def _combine_kernel(pos_ref, ys_hbm, x_ref, gt_ref, wgt_ref, fg_ref, o_ref, buf_ref, sem, *, tt, db, final):
    i = pl.program_id(0)
    pitch = _slab_pitch(db)

    def row_copy(tile, r, k, slot):
        dst = buf_ref.at[slot, pl.ds(pl.multiple_of((k * tt + r) * pitch, SUBLANES), db)]
        return pltpu.make_async_copy(ys_hbm.at[pos_ref[TOP_K * (tile * tt + r) + k]], dst, sem.at[slot])

    def for_rows(fn):
        def body(r, c):
            for k in range(TOP_K):
                fn(r, k)
            return c
        lax.fori_loop(0, tt, body, 0)

    slot = lax.rem(i, 2)

    @pl.when(i == 0)
    def _():
        for_rows(lambda r, k: row_copy(0, r, k, 0).start())

    for_rows(lambda r, k: row_copy(i, r, k, slot).wait())

    @pl.when(i + 1 < pl.num_programs(0))
    def _():
        for_rows(lambda r, k: row_copy(i + 1, r, k, 1 - slot).start())

    wk = [jnp.broadcast_to(wgt_ref[:, k:k + 1], (tt, LANES)) for k in range(TOP_K)]
    sumsq = jnp.zeros((tt, 1), F32)
    for cb in range(db):
        y = wk[0] * buf_ref[slot, pl.ds(cb, tt, stride=pitch), :]
        for k in range(1, TOP_K):
            y = y + wk[k] * buf_ref[slot, pl.ds(k * tt * pitch + cb, tt, stride=pitch), :]
        c0 = cb * LANES
        xn = x_ref[:, c0:c0 + LANES] + gt_ref[0, :, c0:c0 + LANES] * y
        o_ref[:, c0:c0 + LANES] = xn
        if final:
            sumsq = sumsq + jnp.sum(xn * xn, axis=1, keepdims=True)
    if final:
        o_ref[...] = o_ref[...] * lax.rsqrt(sumsq / (db * LANES) + EPS) * fg_ref[...]


def _moe_combine(ys3, pos, top_w, x2, mod_l, gt_blk, S, final_gain, final):
    T, D = x2.shape
    db = D // LANES
    tt = _tile(S, 256, SUBLANES)
    nps = S // tt
    blocks = [((tt, D), F32), ((1, 1, D), F32), ((tt, LANES), F32), ((1, D), F32), ((tt, D), F32)]
    buf_shape = (2, TOP_K * tt * _slab_pitch(db), LANES)
    scratch = [(buf_shape, F32)]
    return pl.pallas_call(
        functools.partial(_combine_kernel, tt=tt, db=db, final=final),
        grid_spec=pltpu.PrefetchScalarGridSpec(
            num_scalar_prefetch=1, grid=(T // tt,),
            in_specs=[pl.BlockSpec(memory_space=pl.ANY),
                      pl.BlockSpec((tt, D), lambda i, pos: (i, 0)),
                      pl.BlockSpec((1, 1, D), lambda i, pos: (i // nps, 0, gt_blk)),
                      pl.BlockSpec((tt, LANES), lambda i, pos: (i, 0)),
                      pl.BlockSpec((1, D), lambda i, pos: (0, 0))],
            out_specs=pl.BlockSpec((tt, D), lambda i, pos: (i, 0)),
            scratch_shapes=[pltpu.VMEM(buf_shape, F32), pltpu.SemaphoreType.DMA((2,))]),
        out_shape=jax.ShapeDtypeStruct((T, D), F32),
        compiler_params=_params(("arbitrary",), blocks, scratch), name="moe_combine",
    )(pos, ys3, x2, mod_l, top_w, final_gain.reshape(1, D))


def _route_tables(top_i, n_experts, tmg, n_rows):
    T = top_i.shape[0]
    e_flat = top_i.reshape(-1)
    onehot = (e_flat[:, None] == jnp.arange(n_experts, dtype=I32)[None, :]).astype(I32)
    csum = jnp.cumsum(onehot, axis=0)
    rank = jnp.take_along_axis(csum, e_flat[:, None], axis=1)[:, 0] - 1
    counts = csum[-1]
    padded = ((counts + tmg - 1) // tmg) * tmg
    ends = jnp.cumsum(padded)
    pos = (ends - padded)[e_flat] + rank
    n_tiles = (ends[-1] // tmg).astype(I32).reshape(1)
    row_tok = jnp.zeros((n_rows,), I32).at[pos].set(jnp.arange(TOP_K * T, dtype=I32) // TOP_K)
    tile_start = jnp.arange(n_rows // tmg, dtype=I32) * tmg
    tile_expert = jnp.sum((tile_start[:, None] >= ends[None, :]).astype(I32), axis=1)
    last_expert = jnp.max(jnp.where(counts > 0, jnp.arange(n_experts, dtype=I32), 0))
    tile_expert = jnp.minimum(tile_expert, last_expert)
    return pos.astype(I32), row_tok, tile_expert, n_tiles


def _moe_ffn(x2, h3, top_i, top_w, w1, w3, w2, mod_l, gt_blk, S, final_gain, final):
    T, D = x2.shape
    E = w1.shape[0]
    tmg = _tile(S, 512, SUBLANES)
    n_rows = TOP_K * T + E * tmg
    pos, row_tok, tile_expert, n_tiles = _route_tables(top_i, E, tmg, n_rows)
    act = _moe_swiglu(h3, row_tok, w1, w3, tile_expert, n_tiles, tmg)
    ys3 = _moe_down(act, w2, tile_expert, n_tiles, tmg)
    return _moe_combine(ys3, pos, top_w, x2, mod_l, gt_blk, S, final_gain, final)


def kernel(x, c, w_ada, b_ada, norm_mix, norm_ffn, w_in, w_pool, pool_scale, lb_logits, hg_norm, w_br_pool, w_br_sb,
           w_br_hg, w_out, ffn_w1, ffn_w3, ffn_w2, w_router, b_router, moe_w1, moe_w3, moe_w2, final_norm):
    B, S, D = x.shape
    depth = w_in.shape[0]
    pw = pool_scale.shape[1]
    sbw = w_br_sb.shape[1]
    hgw = w_br_hg.shape[1]
    c_q_sb = pw
    c_k_sb = c_q_sb + sbw
    c_v_sb = c_k_sb + sbw
    c_q_hg = c_v_sb + sbw
    c_f_hg = c_q_hg + hgw
    c_i_hg = c_f_hg + hgw
    c_g_hg = c_i_hg + hgw
    c_gate = c_g_hg + hgw

    x2 = x.reshape(B * S, D)
    mod = _ada_mod(c, w_ada, b_ada)
    ffn_w2_bf16 = ffn_w2.astype(BF16)
    w_br_pool, w_br_sb, w_br_hg, w_out = (w.astype(BF16) for w in (w_br_pool, w_br_sb, w_br_hg, w_out))
    normed = False
    for l in range(depth):
        mod_l = mod[l].reshape(B, 1, N_MOD * D)
        h = _norm_mod(x2, norm_mix[l], mod_l, 1, 0, S)
        proj = _proj_matmul(h, w_in, l, ACT_DTYPE)
        y_pool = _pool_mixer(proj, w_pool[l].astype(BF16), pool_scale[l], B, S)
        y_sb = _sb_attention(proj, c_q_sb, c_k_sb, c_v_sb, sbw // HEAD, B, S)
        y_hg = _hgrn2(proj, c_q_hg, c_f_hg, c_i_hg, c_g_hg, lb_logits, hg_norm[l], l, hgw // HEAD, B, S)
        merged = _merge_matmul(y_pool, y_sb, y_hg, w_br_pool, w_br_sb, w_br_hg, l, proj, c_gate)
        x2 = _resid_matmul(merged, w_out, l, x2, mod_l, 2, S)
        j = l // 2
        if l % 2 == 0:
            h = _norm_mod(x2, norm_ffn[l], mod_l, 4, 3, S)
            act = _swiglu_matmul(h, ffn_w1, ffn_w3, j)
            x2 = _resid_matmul(act, ffn_w2_bf16, j, x2, mod_l, 5, S, tk_pref=5632)
        else:
            h3, idx, wgt = _norm_mod(x2, norm_ffn[l], mod_l, 4, 3, S, router=(w_router[j], b_router[j]))
            normed = l == depth - 1
            x2 = _moe_ffn(x2, h3, idx[:, :TOP_K], wgt, moe_w1[j].astype(BF16), moe_w3[j].astype(BF16),
                          moe_w2[j].astype(BF16), mod_l, 5, S, final_norm, normed)
    if not normed:
        x2 = _final_norm(x2, final_norm)
    return x2.reshape(B, S, D)
```

```python
import functools
import math

import jax
import jax.numpy as jnp
from jax import lax
from jax.experimental import pallas as pl
from jax.experimental.pallas import tpu as pltpu

F32 = jnp.float32
BF16 = jnp.bfloat16
I32 = jnp.int32

EPS = 1e-6
HEAD = 128
POOL_WINDOWS = (2, 4, 8, 16)
POOL_HALO = 16
HG_CHUNK = 64
HG_SUB = 16
N_MOD = 6
TOP_K = 2
WAIT_UNROLL = 32
LANES = 128
SUBLANES = 8
VMEM_CAP = 60000 * 1024
ACT_DTYPE = BF16


def _tile(n, pref, mult=LANES):
    if n <= pref:
        return n
    t = (pref // mult) * mult
    while t >= mult:
        if n % t == 0:
            return t
        t -= mult
    return n


def _nbytes(shape, dtype):
    n = 1
    for s in shape:
        n *= s
    return n * jnp.dtype(dtype).itemsize


def _params(sems, blocks, scratch=()):
    need = 2 * sum(_nbytes(s, d) for s, d in blocks) + sum(_nbytes(s, d) for s, d in scratch)
    limit = min(VMEM_CAP, max(32 * 1024 * 1024, need + need // 4 + (4 << 20)))
    return pltpu.CompilerParams(dimension_semantics=sems, vmem_limit_bytes=int(limit))


def _split3(x):
    hi = x.astype(BF16)
    r1 = x - hi.astype(F32)
    mid = r1.astype(BF16)
    lo = (r1 - mid.astype(F32)).astype(BF16)
    return hi, mid, lo


def _dot(a, b):
    return jnp.dot(a, b, preferred_element_type=F32)


def _ada_kernel(c_ref, w_ref, b_ref, o_ref):
    ca = jax.nn.silu(c_ref[...])
    o_ref[0] = _dot(ca.astype(BF16), w_ref[0].astype(BF16)) + b_ref[0]


def _ada_mod(c, w_ada, b_ada):
    L, D, N = w_ada.shape
    B = c.shape[0]
    tn = _tile(N, 1024)
    blocks = [((B, D), F32), ((1, D, tn), w_ada.dtype), ((1, 1, tn), F32), ((1, B, tn), F32), ((D, tn), BF16)]
    return pl.pallas_call(
        _ada_kernel,
        grid=(L, N // tn),
        in_specs=[pl.BlockSpec((B, D), lambda l, j: (0, 0)),
                  pl.BlockSpec((1, D, tn), lambda l, j: (l, 0, j)),
                  pl.BlockSpec((1, 1, tn), lambda l, j: (l, 0, j))],
        out_specs=pl.BlockSpec((1, B, tn), lambda l, j: (l, 0, j)),
        out_shape=jax.ShapeDtypeStruct((L, B, N), F32),
        compiler_params=_params(("parallel", "parallel"), blocks),
        name="ada_mod",
    )(c, w_ada, b_ada.reshape(L, 1, N))


def _norm_mod_value(x_ref, g_ref, sc_ref, sh_ref):
    x = x_ref[...]
    ms = jnp.mean(x * x, axis=-1, keepdims=True)
    y = x * lax.rsqrt(ms + EPS) * g_ref[...]
    return y * (1.0 + sc_ref[0]) + sh_ref[0]


def _norm_mod_kernel(x_ref, g_ref, sc_ref, sh_ref, o_ref):
    o_ref[...] = _norm_mod_value(x_ref, g_ref, sc_ref, sh_ref).astype(o_ref.dtype)


def _norm_mod_router_kernel(x_ref, g_ref, sc_ref, sh_ref, wr_ref, br_ref, o_ref, idx_ref, wgt_ref, slab_ref, *,
                            n_experts):
    h = _norm_mod_value(x_ref, g_ref, sc_ref, sh_ref)
    _rows_to_slabs(h, slab_ref, o_ref)
    h_hi = h.astype(BF16)
    h_lo = (h - h_hi.astype(F32)).astype(BF16)
    logits = (_dot(h_hi, wr_ref[0]) + _dot(h_hi, wr_ref[1]) + _dot(h_lo, wr_ref[0])
              + _dot(h_hi, wr_ref[2]) + _dot(h_lo, wr_ref[1])) + br_ref[...]
    lane = lax.broadcasted_iota(I32, logits.shape, 1)
    lanef = lane.astype(F32)
    lg = jnp.where(lane < n_experts, logits, -jnp.inf)
    m1 = jnp.max(lg, axis=-1, keepdims=True)
    i1 = jnp.min(jnp.where(lg == m1, lanef, float(LANES)), axis=-1, keepdims=True)
    lg2 = jnp.where(lanef == i1, -jnp.inf, lg)
    m2 = jnp.max(lg2, axis=-1, keepdims=True)
    i2 = jnp.min(jnp.where(lg2 == m2, lanef, float(LANES)), axis=-1, keepdims=True)
    e2 = jnp.exp(m2 - m1)
    w1 = 1.0 / (1.0 + e2)
    w2 = e2 / (1.0 + e2)
    idx_ref[...] = jnp.where(lane == 0, i1, jnp.where(lane == 1, i2, 0.0)).astype(I32)
    wgt_ref[...] = jnp.where(lane == 0, w1, jnp.where(lane == 1, w2, 0.0))


def _norm_mod(x2, gain, mod_l, sc_blk, sh_blk, S, router=None):
    T, D = x2.shape
    tm = _tile(S, 256, SUBLANES)
    nps = S // tm
    in_specs = [pl.BlockSpec((tm, D), lambda i: (i, 0)),
                pl.BlockSpec((1, D), lambda i: (0, 0)),
                pl.BlockSpec((1, 1, D), lambda i: (i // nps, 0, sc_blk)),
                pl.BlockSpec((1, 1, D), lambda i: (i // nps, 0, sh_blk))]
    blocks = [((tm, D), F32), ((1, D), F32), ((1, 1, D), F32), ((1, 1, D), F32), ((tm, D), ACT_DTYPE)]
    h_spec = pl.BlockSpec((tm, D), lambda i: (i, 0))
    h_shape = jax.ShapeDtypeStruct((T, D), ACT_DTYPE)
    if router is None:
        return pl.pallas_call(
            _norm_mod_kernel, grid=(T // tm,), in_specs=in_specs, out_specs=h_spec, out_shape=h_shape,
            compiler_params=_params(("parallel",), blocks), name="norm_mod",
        )(x2, gain.reshape(1, D), mod_l, mod_l)
    w_router, b_router = router
    E = w_router.shape[1]
    db = _slab_pitch(D // LANES)
    wr = jnp.zeros((D, LANES), F32).at[:, :E].set(w_router)
    wr3 = jnp.stack(_split3(wr))
    br = jnp.zeros((1, LANES), F32).at[0, :E].set(b_router)
    in_specs += [pl.BlockSpec((3, D, LANES), lambda i: (0, 0, 0)), pl.BlockSpec((1, LANES), lambda i: (0, 0))]
    blocks += [((3, D, LANES), BF16), ((tm, LANES), F32), ((tm, LANES), F32), ((tm, D), F32)]
    scratch = [((tm * db, LANES), F32)]
    lane_spec = pl.BlockSpec((tm, LANES), lambda i: (i, 0))
    return pl.pallas_call(
        functools.partial(_norm_mod_router_kernel, n_experts=E),
        grid=(T // tm,), in_specs=in_specs,
        out_specs=[pl.BlockSpec((tm, db, LANES), lambda i: (i, 0, 0)), lane_spec, lane_spec],
        out_shape=[jax.ShapeDtypeStruct((T, db, LANES), F32), jax.ShapeDtypeStruct((T, LANES), I32),
                   jax.ShapeDtypeStruct((T, LANES), F32)],
        scratch_shapes=[pltpu.VMEM((tm * db, LANES), F32)],
        compiler_params=_params(("parallel",), blocks, scratch), name="norm_mod_router",
    )(x2, gain.reshape(1, D), mod_l, mod_l, wr3, br)


def _final_norm_kernel(x_ref, g_ref, o_ref):
    x = x_ref[...]
    ms = jnp.mean(x * x, axis=-1, keepdims=True)
    o_ref[...] = x * lax.rsqrt(ms + EPS) * g_ref[...]


def _final_norm(x2, gain):
    T, D = x2.shape
    tm = _tile(T, 256, SUBLANES)
    blocks = [((tm, D), F32), ((1, D), F32), ((tm, D), F32)]
    return pl.pallas_call(
        _final_norm_kernel, grid=(T // tm,),
        in_specs=[pl.BlockSpec((tm, D), lambda i: (i, 0)), pl.BlockSpec((1, D), lambda i: (0, 0))],
        out_specs=pl.BlockSpec((tm, D), lambda i: (i, 0)),
        out_shape=jax.ShapeDtypeStruct((T, D), F32),
        compiler_params=_params(("parallel",), blocks), name="final_norm",
    )(x2, gain.reshape(1, D))


def _proj_kernel(a_ref, w_ref, o_ref):
    o_ref[...] = _dot(a_ref[...], w_ref[0].astype(BF16)).astype(o_ref.dtype)


def _proj_matmul(a, w, l, out_dtype):
    M, K = a.shape
    N = w.shape[2]
    tm, tn = _tile(M, 2048), _tile(N, 512)
    blocks = [((K, tn), w.dtype), ((tm, tn), out_dtype)]
    scratch = [((tm, K), a.dtype), ((K, tn), BF16)]
    return pl.pallas_call(
        _proj_kernel, grid=(M // tm, N // tn),
        in_specs=[pl.BlockSpec((tm, K), lambda i, j: (i, 0), pipeline_mode=pl.Buffered(1)),
                  pl.BlockSpec((1, K, tn), lambda i, j: (l, 0, j))],
        out_specs=pl.BlockSpec((tm, tn), lambda i, j: (i, j)),
        out_shape=jax.ShapeDtypeStruct((M, N), out_dtype),
        compiler_params=_params(("parallel", "arbitrary"), blocks, scratch), name="proj_matmul",
    )(a, w)


def _swiglu_kernel(a_ref, w1_ref, w3_ref, o_ref):
    a = a_ref[...]
    up = _dot(a, w3_ref[0].astype(BF16))
    o_ref[...] = (jax.nn.silu(_dot(a, w1_ref[0].astype(BF16))) * up).astype(o_ref.dtype)


def _swiglu_matmul(a, w1, w3, l):
    M, K = a.shape
    N = w1.shape[2]
    tm, tn = _tile(M, 2048), _tile(N, 256)
    blocks = [((K, tn), w1.dtype), ((K, tn), w3.dtype), ((tm, tn), ACT_DTYPE)]
    scratch = [((tm, K), a.dtype), ((K, tn), BF16), ((K, tn), BF16)]
    return pl.pallas_call(
        _swiglu_kernel, grid=(M // tm, N // tn),
        in_specs=[pl.BlockSpec((tm, K), lambda i, j: (i, 0), pipeline_mode=pl.Buffered(1)),
                  pl.BlockSpec((1, K, tn), lambda i, j: (l, 0, j)),
                  pl.BlockSpec((1, K, tn), lambda i, j: (l, 0, j))],
        out_specs=pl.BlockSpec((tm, tn), lambda i, j: (i, j)),
        out_shape=jax.ShapeDtypeStruct((M, N), ACT_DTYPE),
        compiler_params=_params(("parallel", "arbitrary"), blocks, scratch), name="swiglu_matmul",
    )(a, w1, w3)


def _resid_kernel(a_ref, w_ref, x_ref, gt_ref, o_ref, acc_ref, *, nk):
    part = _dot(a_ref[...], w_ref[0].astype(BF16))
    if nk == 1:
        o_ref[...] = x_ref[...] + gt_ref[0] * part
        return
    k = pl.program_id(2)

    @pl.when(k == 0)
    def _():
        acc_ref[...] = part

    @pl.when(jnp.logical_and(k > 0, k < nk - 1))
    def _():
        acc_ref[...] += part

    @pl.when(k == nk - 1)
    def _():
        o_ref[...] = x_ref[...] + gt_ref[0] * (acc_ref[...] + part)


def _resid_matmul(a, w, l, x2, mod_l, gt_blk, S, tk_pref=4096):
    M, K = a.shape
    N = w.shape[2]
    tk = _tile(K, tk_pref)
    nk = K // tk
    tm, tn = _tile(S, 1024), _tile(N, 512)
    nps = S // tm
    gpb = N // tn
    blocks = [((tm, tk), a.dtype), ((tk, tn), w.dtype), ((tm, tn), F32), ((1, 1, tn), F32), ((tm, tn), F32)]
    scratch = [((tm, tn), F32), ((tk, tn), BF16)]
    return pl.pallas_call(
        functools.partial(_resid_kernel, nk=nk), grid=(M // tm, N // tn, nk),
        in_specs=[pl.BlockSpec((tm, tk), lambda i, j, k: (i, k)),
                  pl.BlockSpec((1, tk, tn), lambda i, j, k: (l, k, j)),
                  pl.BlockSpec((tm, tn), lambda i, j, k: (i, j)),
                  pl.BlockSpec((1, 1, tn), lambda i, j, k: (i // nps, 0, gt_blk * gpb + j))],
        out_specs=pl.BlockSpec((tm, tn), lambda i, j, k: (i, j)),
        out_shape=jax.ShapeDtypeStruct((M, N), F32),
        scratch_shapes=[pltpu.VMEM((tm, tn), F32)],
        compiler_params=_params(("parallel", "arbitrary", "arbitrary"), blocks, scratch), name="resid_matmul",
    )(a, w, x2, mod_l)


def _merge_kernel(yp_ref, ys_ref, yh_ref, wp_ref, ws_ref, wh_ref, g0_ref, g1_ref, g2_ref, o_ref):
    def gate(g_ref):
        return jax.nn.sigmoid(g_ref[...].astype(F32))
    m = (gate(g0_ref) * _dot(yp_ref[...], wp_ref[0].astype(BF16))
         + gate(g1_ref) * _dot(ys_ref[...], ws_ref[0].astype(BF16))
         + gate(g2_ref) * _dot(yh_ref[...], wh_ref[0].astype(BF16)))
    o_ref[...] = m.astype(o_ref.dtype)


def _merge_matmul(y_pool, y_sb, y_hg, w_p, w_s, w_h, l, proj, gate_col0):
    M = y_pool.shape[0]
    D = w_p.shape[2]
    tm, tn = _tile(M, 1024), _tile(math.gcd(D, gate_col0), 512)
    g0 = gate_col0 // tn
    gpb = D // tn
    ks = (y_pool.shape[1], y_sb.shape[1], y_hg.shape[1])
    blocks = ([((tm, k), ACT_DTYPE) for k in ks] + [((k, tn), w.dtype) for k, w in zip(ks, (w_p, w_s, w_h))]
              + [((tm, tn), proj.dtype)] * 3 + [((tm, tn), ACT_DTYPE)])
    a_specs = [pl.BlockSpec((tm, k), lambda i, j: (i, 0)) for k in ks]
    w_specs = [pl.BlockSpec((1, k, tn), lambda i, j: (l, 0, j)) for k in ks]
    g_specs = [pl.BlockSpec((tm, tn), functools.partial(lambda i, j, b: (i, g0 + b * gpb + j), b=b)) for b in range(3)]
    return pl.pallas_call(
        _merge_kernel, grid=(M // tm, D // tn),
        in_specs=a_specs + w_specs + g_specs,
        out_specs=pl.BlockSpec((tm, tn), lambda i, j: (i, j)),
        out_shape=jax.ShapeDtypeStruct((M, D), ACT_DTYPE),
        compiler_params=_params(("parallel", "arbitrary"), blocks), name="merge_matmul",
    )(y_pool, y_sb, y_hg, w_p, w_s, w_h, proj, proj, proj)


def _pool_kernel(u_ref, wp_ref, ps_ref, o_ref, ext_ref, *, ts, pc):
    si = pl.program_id(1)

    @pl.when(si == 0)
    def _():
        ext_ref[0:POOL_HALO, :] = jnp.zeros((POOL_HALO, ext_ref.shape[1]), F32)

    u = u_ref[...].astype(F32)
    ext_ref[POOL_HALO:POOL_HALO + ts, :] = u
    pos = si * ts + lax.broadcasted_iota(I32, (ts, 1), 0)
    for g, w in enumerate(POOL_WINDOWS):
        c0, c1 = g * pc, (g + 1) * pc
        ug = u[:, c0:c1]
        acc = ug
        for j in range(1, w):
            acc = acc + ext_ref[POOL_HALO - j:POOL_HALO - j + ts, c0:c1]
        cnt = jnp.minimum(pos + 1, w).astype(F32)
        pooled = acc / cnt - ug
        y = _dot(pooled.astype(BF16), wp_ref[g]) * ps_ref[:, c0:c1]
        o_ref[:, c0:c1] = y.astype(o_ref.dtype)
    ext_ref[0:POOL_HALO, :] = ext_ref[ts:ts + POOL_HALO, :]


def _pool_mixer(proj, w_pool, pool_scale, B, S):
    G, pc, _ = w_pool.shape
    pw = G * pc
    ts = _tile(S, 512, POOL_HALO)
    ns = S // ts
    blocks = [((ts, pw), proj.dtype), ((G, pc, pc), BF16), ((1, pw), F32), ((ts, pw), ACT_DTYPE)]
    scratch = [((ts + POOL_HALO, pw), F32)]
    return pl.pallas_call(
        functools.partial(_pool_kernel, ts=ts, pc=pc), grid=(B, ns),
        in_specs=[pl.BlockSpec((ts, pw), lambda b, s: (b * ns + s, 0)),
                  pl.BlockSpec((G, pc, pc), lambda b, s: (0, 0, 0)),
                  pl.BlockSpec((1, pw), lambda b, s: (0, 0))],
        out_specs=pl.BlockSpec((ts, pw), lambda b, s: (b * ns + s, 0)),
        out_shape=jax.ShapeDtypeStruct((B * S, pw), ACT_DTYPE),
        scratch_shapes=[pltpu.VMEM((ts + POOL_HALO, pw), F32)],
        compiler_params=_params(("parallel", "arbitrary"), blocks, scratch), name="pool_mixer",
    )(proj, w_pool, pool_scale.reshape(1, pw))


def _sb_kernel(q_ref, k_ref, v_ref, o_ref, *, tq, hb):
    qi = pl.program_id(2)
    scale = HEAD ** -0.5
    rows = lax.broadcasted_iota(I32, (tq, tq), 0)
    cols = lax.broadcasted_iota(I32, (tq, tq), 1)
    later = (rows > cols).astype(BF16)
    below = cols < rows
    qs = [q_ref[:, h * HEAD:(h + 1) * HEAD].astype(BF16) for h in range(hb)]

    def tiles(kj, carry, diagonal):
        k0 = pl.multiple_of(kj * tq, tq)
        heads = range(hb)
        zs = [lax.dot_general(qs[h], k_ref[pl.ds(k0, tq), h * HEAD:(h + 1) * HEAD].astype(BF16),
                              (((1,), (1,)), ((), ())), preferred_element_type=F32) * scale for h in heads]
        log_betas, log_keeps, splits = [], [], []
        for h in heads:
            z = zs[h]
            log_beta = jnp.minimum(z, 0.0) - jnp.log(1.0 + jnp.exp(-jnp.abs(z)))
            log_keep = log_beta - z
            if diagonal:
                log_keep = jnp.where(below, log_keep, 0.0)
            hi = log_keep.astype(BF16)
            lo = (log_keep - hi.astype(F32)).astype(BF16)
            log_betas.append(log_beta)
            log_keeps.append(log_keep)
            splits.append((hi, lo))
        afters = [_dot(splits[h][0], later) + _dot(splits[h][1], later) + carry[2 * h + 1] for h in heads]
        weights = []
        for h in heads:
            a = jnp.exp(log_betas[h] + afters[h])
            if diagonal:
                a = jnp.where(below, a, 0.0)
            weights.append(a.astype(BF16))
        out = []
        for h in heads:
            vs = v_ref[pl.ds(k0, tq), h * HEAD:(h + 1) * HEAD].astype(BF16)
            out.append(carry[2 * h] + _dot(weights[h], vs))
            out.append(carry[2 * h + 1] + jnp.sum(log_keeps[h], axis=1, keepdims=True))
        return tuple(out)

    zero = (jnp.zeros((tq, HEAD), F32), jnp.zeros((tq, 1), F32))
    carry = tiles(qi, zero * hb, True)
    carry = lax.fori_loop(1, qi + 1, lambda jj, c: tiles(qi - jj, c, False), carry)
    for h in range(hb):
        o_ref[:, h * HEAD:(h + 1) * HEAD] = carry[2 * h].astype(o_ref.dtype)


def _heads_per_step(n_heads, col0s, prefs):
    for hb in prefs:
        if n_heads % hb == 0 and all(c % (hb * HEAD) == 0 for c in col0s):
            return hb
    return 1


def _sb_attention(proj, q_col0, k_col0, v_col0, n_heads, B, S):
    tq = _tile(S, 256)
    nq = S // tq
    hb = _heads_per_step(n_heads, (q_col0, k_col0, v_col0), (4, 2))
    w = hb * HEAD
    qb, kb, vb = q_col0 // w, k_col0 // w, v_col0 // w
    blocks = [((tq, w), proj.dtype), ((S, w), proj.dtype), ((S, w), proj.dtype), ((tq, w), ACT_DTYPE)]
    return pl.pallas_call(
        functools.partial(_sb_kernel, tq=tq, hb=hb), grid=(B, n_heads // hb, nq),
        in_specs=[pl.BlockSpec((tq, w), lambda b, h, i: (b * nq + i, qb + h)),
                  pl.BlockSpec((S, w), lambda b, h, i: (b, kb + h)),
                  pl.BlockSpec((S, w), lambda b, h, i: (b, vb + h))],
        out_specs=pl.BlockSpec((tq, w), lambda b, h, i: (b * nq + i, h)),
        out_shape=jax.ShapeDtypeStruct((B * S, n_heads * HEAD), ACT_DTYPE),
        compiler_params=_params(("parallel", "parallel", "arbitrary"), blocks), name="sb_attention",
    )(proj, proj, proj)


def _hg_kernel(q_ref, f_ref, i_ref, g_ref, lbl_ref, gn_ref, o_ref, st_ref, b_ref, kk_ref, *, layer, n_chunks, hb):
    C = HG_CHUNK
    si = pl.program_id(2)

    @pl.when(si == 0)
    def _():
        st_ref[...] = jnp.zeros(st_ref.shape, F32)

    lbl = lbl_ref[...]
    e = jnp.exp(lbl - jnp.max(lbl, axis=0, keepdims=True))
    p = e / jnp.sum(e, axis=0, keepdims=True)
    lb = jnp.zeros((1, hb * HEAD), F32)
    for j in range(1, layer + 1):
        lb = lb + p[j:j + 1, :]
    gn = gn_ref[...]
    rows = lax.broadcasted_iota(I32, (C, C), 0)
    cols = lax.broadcasted_iota(I32, (C, C), 1)
    incl = (rows >= cols).astype(BF16)
    lane = lax.broadcasted_iota(I32, (SUBLANES, C), 1)
    sub = lax.broadcasted_iota(I32, (SUBLANES, HEAD), 0)
    nb = C // HG_SUB
    pairs = [(bi, bj) for bi in range(nb) for bj in range(bi)]

    def gates(h, r0, c):
        hs = slice(h * HEAD, (h + 1) * HEAD)
        c["hs"] = hs
        qh = q_ref[pl.ds(r0, C), hs].astype(F32)
        fh = f_ref[pl.ds(r0, C), hs].astype(F32)
        c["ih"] = i_ref[pl.ds(r0, C), hs].astype(BF16)
        c["gh"] = g_ref[pl.ds(r0, C), hs].astype(F32)
        lbh = lb[:, hs]
        f = lbh + (1.0 - lbh) * jax.nn.sigmoid(fh)
        c["kk"] = 1.0 - f
        c["qa"] = jax.nn.silu(qh)
        c["logf"] = _split3(jnp.log(f))

    def decay(h, r0, c):
        hi, mid, lo = c["logf"]
        c["b"] = _dot(incl, hi) + _dot(incl, mid) + _dot(incl, lo)

    def factors(h, r0, c):
        b, kk, qa = c["b"], c["kk"], c["qa"]
        b_ref[h] = b
        kk_ref[h] = kk
        edge = [jnp.zeros((1, HEAD), F32)] + [b[m * HG_SUB - 1:m * HG_SUB, :] for m in range(1, nb + 1)]
        b_start = jnp.concatenate([jnp.broadcast_to(edge[m], (HG_SUB, HEAD)) for m in range(nb)], axis=0)
        b_end = jnp.concatenate([jnp.broadcast_to(edge[m + 1], (HG_SUB, HEAD)) for m in range(nb)], axis=0)
        qd_sub = qa * jnp.exp(b - b_start)
        kd_sub = kk * jnp.exp(b_end - b)
        lhs = jnp.concatenate([qd_sub[bi * HG_SUB:(bi + 1) * HG_SUB, :] * jnp.exp(edge[bi] - edge[bj + 1])
                               for bi, bj in pairs], axis=0)
        c["lhs"] = lhs.astype(BF16)
        c["rhs"] = kd_sub.astype(BF16)
        c["qd"] = (qd_sub * jnp.exp(b_start)).astype(BF16)
        c["kd"] = (kd_sub * jnp.exp(edge[nb] - b_end)).astype(BF16)
        c["carry"] = jnp.exp(edge[nb])

    def cross_scores(h, r0, c):
        c["cross"] = lax.dot_general(c["lhs"], c["rhs"], (((1,), (1,)), ((), ())),
                                     preferred_element_type=F32)

    def scores(h, r0, c):
        b, qa, cross = c["b"], c["qa"], c["cross"]
        pieces = []
        for g in range(C // SUBLANES):
            t0 = g * SUBLANES
            bi = t0 // HG_SUB
            bt = b[t0:t0 + SUBLANES, :]
            qt = qa[t0:t0 + SUBLANES, :]
            sc = jnp.zeros((SUBLANES, C), F32)
            for p, (pi, pj) in enumerate(pairs):
                if pi == bi:
                    r = p * HG_SUB + (t0 - bi * HG_SUB)
                    in_block = jnp.logical_and(lane >= pj * HG_SUB, lane < (pj + 1) * HG_SUB)
                    sc = jnp.where(in_block, cross[r:r + SUBLANES, :], sc)
            for s in range(bi * HG_SUB, t0 + SUBLANES):
                d = bt - b_ref[h, pl.ds(s, 1), :]
                if s >= t0:
                    d = jnp.where(sub + t0 >= s, d, -jnp.inf)
                wgt = qt * kk_ref[h, pl.ds(s, 1), :] * jnp.exp(d)
                col = jnp.sum(wgt, axis=1, keepdims=True)
                sc = jnp.where(lane == s, col, sc)
            pieces.append(sc)
        c["scores"] = jnp.concatenate(pieces, axis=0).astype(BF16)

    def outputs(h, r0, c):
        ih = c["ih"]
        st = st_ref[h]
        o = _dot(c["scores"], ih)
        c["o"] = o + lax.dot_general(c["qd"], st.astype(BF16), (((1,), (1,)), ((), ())), preferred_element_type=F32)
        st_ref[h] = st * c["carry"] + lax.dot_general(ih, c["kd"], (((0,), (0,)), ((), ())),
                                                      preferred_element_type=F32)

    def finish(h, r0, c):
        o = c["o"]
        y = o * lax.rsqrt(jnp.mean(o * o, axis=-1, keepdims=True) + EPS) * gn
        o_ref[pl.ds(r0, C), c["hs"]] = (y * jax.nn.silu(c["gh"])).astype(o_ref.dtype)

    def chunk(ci, carry):
        r0 = pl.multiple_of(ci * C, C)
        live = [dict() for _ in range(hb)]
        for stage in (gates, decay, factors, cross_scores, scores, outputs, finish):
            for h in range(hb):
                stage(h, r0, live[h])
        return carry

    lax.fori_loop(0, n_chunks, chunk, 0)


def _hgrn2(proj, q_col0, f_col0, i_col0, g_col0, lb_logits, hg_norm, layer, n_heads, B, S):
    ts = _tile(S, 512, HG_CHUNK)
    ns = S // ts
    L = lb_logits.shape[0]
    hb = _heads_per_step(n_heads, (q_col0, f_col0, i_col0, g_col0), (4, 2))
    w = hb * HEAD
    qb, fb, ib, gb = q_col0 // w, f_col0 // w, i_col0 // w, g_col0 // w

    def col_spec(cb):
        return pl.BlockSpec((ts, w), lambda b, h, s: (b * ns + s, cb + h))

    blocks = [((ts, w), proj.dtype)] * 4 + [((L, w), F32), ((1, HEAD), F32), ((ts, w), ACT_DTYPE)]
    scratch = [((hb, HEAD, HEAD), F32), ((hb, HG_CHUNK, HEAD), F32), ((hb, HG_CHUNK, HEAD), F32)]
    return pl.pallas_call(
        functools.partial(_hg_kernel, layer=layer, n_chunks=ts // HG_CHUNK, hb=hb), grid=(B, n_heads // hb, ns),
        in_specs=[col_spec(qb), col_spec(fb), col_spec(ib), col_spec(gb),
                  pl.BlockSpec((L, w), lambda b, h, s: (0, h)),
                  pl.BlockSpec((1, HEAD), lambda b, h, s: (0, 0))],
        out_specs=pl.BlockSpec((ts, w), lambda b, h, s: (b * ns + s, h)),
        out_shape=jax.ShapeDtypeStruct((B * S, n_heads * HEAD), ACT_DTYPE),
        scratch_shapes=[pltpu.VMEM(s, d) for s, d in scratch],
        compiler_params=_params(("parallel", "parallel", "arbitrary"), blocks, scratch), name="hgrn2",
    )(proj, proj, proj, proj, lb_logits, hg_norm.reshape(1, HEAD))


def _slab_pitch(db):
    return db if (db // SUBLANES) % 2 == 1 else db + SUBLANES


def _moe_swiglu_kernel(te_ref, nt_ref, tok_ref, h_hbm, w1_ref, w3_ref, o_ref, buf_ref, a_ref, sem, *, tmg, db, nj):
    i = pl.program_id(0)
    j = pl.program_id(1)
    nt = nt_ref[0]
    pitch = _slab_pitch(db)

    def row_copy(tile, r, slot):
        dst = buf_ref.at[slot, pl.ds(pl.multiple_of(r * pitch, SUBLANES), db)]
        return pltpu.make_async_copy(h_hbm.at[tok_ref[tile * tmg + r], pl.ds(0, db)], dst, sem.at[slot])

    def start_gather(tile, slot, lo, hi):
        lax.fori_loop(lo, hi, lambda r, c: (row_copy(tile, r, slot).start(), c)[1], 0)

    def wait_gather(tile, slot):
        lax.fori_loop(0, tmg, lambda r, c: (row_copy(tile, r, slot).wait(), c)[1], 0, unroll=WAIT_UNROLL)

    slot = lax.rem(i, 2)

    @pl.when(j == 0)
    def _():
        @pl.when(i == 0)
        def _():
            start_gather(0, 0, 0, tmg)

        @pl.when(i < nt)
        def _():
            wait_gather(i, slot)
            for cb in range(db):
                a_ref[:, cb * LANES:(cb + 1) * LANES] = buf_ref[slot, pl.ds(cb, tmg, stride=pitch), :].astype(BF16)

    @pl.when(i + 1 < nt)
    def _():
        share = -(-tmg // nj)
        start_gather(i + 1, 1 - slot, jnp.minimum(j * share, tmg), jnp.minimum((j + 1) * share, tmg))

    @pl.when(i < nt)
    def _():
        a = a_ref[...]
        o_ref[...] = (jax.nn.silu(_dot(a, w1_ref[0])) * _dot(a, w3_ref[0])).astype(o_ref.dtype)

    @pl.when(i >= nt)
    def _():
        o_ref[...] = jnp.zeros(o_ref.shape, o_ref.dtype)


def _moe_swiglu(h3, row_tok, w1, w3, tile_expert, n_tiles, tmg):
    P = row_tok.shape[0]
    K = w1.shape[1]
    db = K // LANES
    N = w1.shape[2]
    tn = _tile(N, 512)
    blocks = [((1, K, tn), BF16), ((1, K, tn), BF16), ((tmg, tn), ACT_DTYPE)]
    buf_shape = (2, tmg * _slab_pitch(db), LANES)
    scratch = [(buf_shape, F32), ((tmg, K), BF16)]

    def w_map(i, j, te, nt, tok):
        return (te[i], 0, jnp.where(i < nt[0], j, N // tn - 1))

    return pl.pallas_call(
        functools.partial(_moe_swiglu_kernel, tmg=tmg, db=db, nj=N // tn),
        grid_spec=pltpu.PrefetchScalarGridSpec(
            num_scalar_prefetch=3, grid=(P // tmg, N // tn),
            in_specs=[pl.BlockSpec(memory_space=pl.ANY), pl.BlockSpec((1, K, tn), w_map), pl.BlockSpec((1, K, tn), w_map)],
            out_specs=pl.BlockSpec((tmg, tn), lambda i, j, te, nt, tok: (i, j)),
            scratch_shapes=[pltpu.VMEM(buf_shape, F32), pltpu.VMEM((tmg, K), BF16),
                            pltpu.SemaphoreType.DMA((2,))]),
        out_shape=jax.ShapeDtypeStruct((P, N), ACT_DTYPE),
        compiler_params=_params(("arbitrary", "arbitrary"), blocks, scratch), name="moe_swiglu",
    )(tile_expert, n_tiles, row_tok, h3, w1, w3)


def _rows_to_slabs(val, slab_ref, o_ref):
    m, n = o_ref.shape[0], o_ref.shape[1]
    nv = val.shape[1] // LANES
    for c in range(nv):
        slab_ref[pl.ds(c, m, stride=n), :] = val[:, c * LANES:(c + 1) * LANES]
    for c in range(nv, n):
        slab_ref[pl.ds(c, m, stride=n), :] = jnp.zeros((m, LANES), val.dtype)
    o_ref[...] = slab_ref[...].reshape(m, n, LANES)


def _moe_down_kernel(te_ref, nt_ref, a_ref, w_ref, o_ref, slab_ref):
    active = pl.program_id(0) < nt_ref[0]

    @pl.when(active)
    def _():
        _rows_to_slabs(_dot(a_ref[...], w_ref[0]), slab_ref, o_ref)

    @pl.when(jnp.logical_not(active))
    def _():
        o_ref[...] = jnp.zeros(o_ref.shape, o_ref.dtype)


def _moe_down(a, w2, tile_expert, n_tiles, tmg):
    P, K = a.shape
    N = w2.shape[2]
    tn = _tile(N, 1024)
    nb = tn // LANES
    blocks = [((tmg, K), a.dtype), ((1, K, tn), BF16), ((tmg, tn), F32)]
    scratch = [((tmg * nb, LANES), F32)]

    def a_map(i, j, te, nt):
        return (jnp.minimum(i, nt[0] - 1), 0)

    def w_map(i, j, te, nt):
        return (te[i], 0, jnp.where(i < nt[0], j, N // tn - 1))

    return pl.pallas_call(
        _moe_down_kernel,
        grid_spec=pltpu.PrefetchScalarGridSpec(
            num_scalar_prefetch=2, grid=(P // tmg, N // tn),
            in_specs=[pl.BlockSpec((tmg, K), a_map), pl.BlockSpec((1, K, tn), w_map)],
            out_specs=pl.BlockSpec((tmg, nb, LANES), lambda i, j, te, nt: (i, j, 0)),
            scratch_shapes=[pltpu.VMEM((tmg * nb, LANES), F32)]),
        out_shape=jax.ShapeDtypeStruct((P, N // LANES, LANES), F32),
        compiler_params=_params(("arbitrary", "arbitrary"), blocks, scratch), name="moe_down",
    )(tile_expert, n_tiles, a, w2)


def _combine_kernel(pos_ref, ys_hbm, x_ref, gt_ref, wgt_ref, fg_ref, o_ref, buf_ref, sem, *, tt, db, final):
    i = pl.program_id(0)
    pitch = _slab_pitch(db)

    def row_copy(tile, r, k, slot):
        dst = buf_ref.at[slot, pl.ds(pl.multiple_of((k * tt + r) * pitch, SUBLANES), db)]
        return pltpu.make_async_copy(ys_hbm.at[pos_ref[TOP_K * (tile * tt + r) + k]], dst, sem.at[slot])

    def for_rows(fn, unroll=1):
        def body(r, c):
            for k in range(TOP_K):
                fn(r, k)
            return c
        lax.fori_loop(0, tt, body, 0, unroll=unroll)

    slot = lax.rem(i, 2)

    @pl.when(i == 0)
    def _():
        for_rows(lambda r, k: row_copy(0, r, k, 0).start())

    for_rows(lambda r, k: row_copy(i, r, k, slot).wait(), unroll=WAIT_UNROLL)

    @pl.when(i + 1 < pl.num_programs(0))
    def _():
        for_rows(lambda r, k: row_copy(i + 1, r, k, 1 - slot).start())

    wk = [jnp.broadcast_to(wgt_ref[:, k:k + 1], (tt, LANES)) for k in range(TOP_K)]
    sumsq = jnp.zeros((tt, 1), F32)
    for cb in range(db):
        y = wk[0] * buf_ref[slot, pl.ds(cb, tt, stride=pitch), :]
        for k in range(1, TOP_K):
            y = y + wk[k] * buf_ref[slot, pl.ds(k * tt * pitch + cb, tt, stride=pitch), :]
        c0 = cb * LANES
        xn = x_ref[:, c0:c0 + LANES] + gt_ref[0, :, c0:c0 + LANES] * y
        o_ref[:, c0:c0 + LANES] = xn
        if final:
            sumsq = sumsq + jnp.sum(xn * xn, axis=1, keepdims=True)
    if final:
        o_ref[...] = o_ref[...] * lax.rsqrt(sumsq / (db * LANES) + EPS) * fg_ref[...]


def _moe_combine(ys3, pos, top_w, x2, mod_l, gt_blk, S, final_gain, final):
    T, D = x2.shape
    db = D // LANES
    tt = _tile(S, 256, SUBLANES)
    nps = S // tt
    blocks = [((tt, D), F32), ((1, 1, D), F32), ((tt, LANES), F32), ((1, D), F32), ((tt, D), F32)]
    buf_shape = (2, TOP_K * tt * _slab_pitch(db), LANES)
    scratch = [(buf_shape, F32)]
    return pl.pallas_call(
        functools.partial(_combine_kernel, tt=tt, db=db, final=final),
        grid_spec=pltpu.PrefetchScalarGridSpec(
            num_scalar_prefetch=1, grid=(T // tt,),
            in_specs=[pl.BlockSpec(memory_space=pl.ANY),
                      pl.BlockSpec((tt, D), lambda i, pos: (i, 0)),
                      pl.BlockSpec((1, 1, D), lambda i, pos: (i // nps, 0, gt_blk)),
                      pl.BlockSpec((tt, LANES), lambda i, pos: (i, 0)),
                      pl.BlockSpec((1, D), lambda i, pos: (0, 0))],
            out_specs=pl.BlockSpec((tt, D), lambda i, pos: (i, 0)),
            scratch_shapes=[pltpu.VMEM(buf_shape, F32), pltpu.SemaphoreType.DMA((2,))]),
        out_shape=jax.ShapeDtypeStruct((T, D), F32),
        compiler_params=_params(("arbitrary",), blocks, scratch), name="moe_combine",
    )(pos, ys3, x2, mod_l, top_w, final_gain.reshape(1, D))


def _route_tables(top_i, n_experts, tmg, n_rows):
    T = top_i.shape[0]
    e_flat = top_i.reshape(-1)
    onehot = (e_flat[:, None] == jnp.arange(n_experts, dtype=I32)[None, :]).astype(I32)
    csum = jnp.cumsum(onehot, axis=0)
    rank = jnp.take_along_axis(csum, e_flat[:, None], axis=1)[:, 0] - 1
    counts = csum[-1]
    padded = ((counts + tmg - 1) // tmg) * tmg
    ends = jnp.cumsum(padded)
    pos = (ends - padded)[e_flat] + rank
    n_tiles = (ends[-1] // tmg).astype(I32).reshape(1)
    row_tok = jnp.zeros((n_rows,), I32).at[pos].set(jnp.arange(TOP_K * T, dtype=I32) // TOP_K)
    tile_start = jnp.arange(n_rows // tmg, dtype=I32) * tmg
    tile_expert = jnp.sum((tile_start[:, None] >= ends[None, :]).astype(I32), axis=1)
    last_expert = jnp.max(jnp.where(counts > 0, jnp.arange(n_experts, dtype=I32), 0))
    tile_expert = jnp.minimum(tile_expert, last_expert)
    return pos.astype(I32), row_tok, tile_expert, n_tiles


def _moe_ffn(x2, h3, top_i, top_w, w1, w3, w2, mod_l, gt_blk, S, final_gain, final):
    T, D = x2.shape
    E = w1.shape[0]
    tmg = _tile(S, 512, SUBLANES)
    n_rows = TOP_K * T + E * tmg
    pos, row_tok, tile_expert, n_tiles = _route_tables(top_i, E, tmg, n_rows)
    act = _moe_swiglu(h3, row_tok, w1, w3, tile_expert, n_tiles, tmg)
    ys3 = _moe_down(act, w2, tile_expert, n_tiles, tmg)
    return _moe_combine(ys3, pos, top_w, x2, mod_l, gt_blk, S, final_gain, final)


def kernel(x, c, w_ada, b_ada, norm_mix, norm_ffn, w_in, w_pool, pool_scale, lb_logits, hg_norm, w_br_pool, w_br_sb,
           w_br_hg, w_out, ffn_w1, ffn_w3, ffn_w2, w_router, b_router, moe_w1, moe_w3, moe_w2, final_norm):
    B, S, D = x.shape
    depth = w_in.shape[0]
    pw = pool_scale.shape[1]
    sbw = w_br_sb.shape[1]
    hgw = w_br_hg.shape[1]
    c_q_sb = pw
    c_k_sb = c_q_sb + sbw
    c_v_sb = c_k_sb + sbw
    c_q_hg = c_v_sb + sbw
    c_f_hg = c_q_hg + hgw
    c_i_hg = c_f_hg + hgw
    c_g_hg = c_i_hg + hgw
    c_gate = c_g_hg + hgw

    x2 = x.reshape(B * S, D)
    mod = _ada_mod(c, w_ada, b_ada)
    ffn_w2_bf16 = ffn_w2.astype(BF16)
    w_br_pool, w_br_sb, w_br_hg, w_out = (w.astype(BF16) for w in (w_br_pool, w_br_sb, w_br_hg, w_out))
    normed = False
    for l in range(depth):
        mod_l = mod[l].reshape(B, 1, N_MOD * D)
        h = _norm_mod(x2, norm_mix[l], mod_l, 1, 0, S)
        proj = _proj_matmul(h, w_in, l, ACT_DTYPE)
        y_pool = _pool_mixer(proj, w_pool[l].astype(BF16), pool_scale[l], B, S)
        y_sb = _sb_attention(proj, c_q_sb, c_k_sb, c_v_sb, sbw // HEAD, B, S)
        y_hg = _hgrn2(proj, c_q_hg, c_f_hg, c_i_hg, c_g_hg, lb_logits, hg_norm[l], l, hgw // HEAD, B, S)
        merged = _merge_matmul(y_pool, y_sb, y_hg, w_br_pool, w_br_sb, w_br_hg, l, proj, c_gate)
        x2 = _resid_matmul(merged, w_out, l, x2, mod_l, 2, S)
        j = l // 2
        if l % 2 == 0:
            h = _norm_mod(x2, norm_ffn[l], mod_l, 4, 3, S)
            act = _swiglu_matmul(h, ffn_w1, ffn_w3, j)
            x2 = _resid_matmul(act, ffn_w2_bf16, j, x2, mod_l, 5, S, tk_pref=5632)
        else:
            h3, idx, wgt = _norm_mod(x2, norm_ffn[l], mod_l, 4, 3, S, router=(w_router[j], b_router[j]))
            normed = l == depth - 1
            x2 = _moe_ffn(x2, h3, idx[:, :TOP_K], wgt, moe_w1[j].astype(BF16), moe_w3[j].astype(BF16),
                          moe_w2[j].astype(BF16), mod_l, 5, S, final_norm, normed)
    if not normed:
        x2 = _final_norm(x2, final_norm)
    return x2.reshape(B, S, D)
```
